```python
import math
import jax, jax.numpy as jnp
from jax import lax
import numpy as np

D_MODEL = 1024
BATCH = 8
SEQ = 4096
DEPTH = 1
DEC_BATCH = 128
DEC_SEQ = 4
PAST_LEN = 8192
PAGE_SIZE = 128

HEAD_DIM = 64
SB_HEADS = 8
DIFF_HEADS = 4
DIFF_V_DIM = 2 * HEAD_DIM
SB_WIDTH = SB_HEADS * HEAD_DIM
DIFF_QK_WIDTH = DIFF_HEADS * 2 * HEAD_DIM
DIFF_V_WIDTH = DIFF_HEADS * DIFF_V_DIM
MIX_WIDTH = SB_WIDTH + DIFF_V_WIDTH
IN_PROJ_WIDTH = 3 * SB_WIDTH + 2 * DIFF_QK_WIDTH + DIFF_V_WIDTH
SPLIT_POINTS = (SB_WIDTH, 2 * SB_WIDTH, 3 * SB_WIDTH,
                3 * SB_WIDTH + DIFF_QK_WIDTH, 3 * SB_WIDTH + 2 * DIFF_QK_WIDTH)
N_EXPERTS = 32
TOP_K = 4
D_FF = D_MODEL
SWIGLU_LIMIT = 7.0
SWIGLU_ALPHA = 1.702
Q_BLOCK = 128
RMS_EPS = 1e-5
NEG_BIG = -1e30

kernel_name = "hymba_stickbreak_diffattn_moe_step"


def rms_norm(x, w):
    xf = x.astype(jnp.float32)
    y = xf * lax.rsqrt(jnp.mean(xf * xf, axis=-1, keepdims=True) + RMS_EPS)
    return (y * w.astype(jnp.float32)).astype(x.dtype)


def ada_params(c, w, b):
    h = jax.nn.silu(c) @ w + b
    return jnp.split(h[:, None, :], 6, axis=-1)


def modulate(h, shift, scale):
    return h * (1 + scale) + shift


def alibi_slopes():
    return 2.0 ** (-8.0 * jnp.arange(1, DIFF_HEADS + 1, dtype=jnp.float32) / DIFF_HEADS)


def in_projection(h, w_in):
    b, t, _ = h.shape
    p = h @ w_in
    sb_q, sb_k, sb_v, d_q, d_k, d_v = jnp.split(p, list(SPLIT_POINTS), axis=-1)
    sb = lambda a: a.reshape(b, t, SB_HEADS, HEAD_DIM)
    dd = lambda a: a.reshape(b, t, DIFF_HEADS, 2 * HEAD_DIM)
    return sb(sb_q), sb(sb_k), sb(sb_v), dd(d_q), dd(d_k), d_v.reshape(b, t, DIFF_HEADS, DIFF_V_DIM)


def stick_breaking_core(q, k, v, q_pos, k_pos):
    z = jnp.einsum('qhd,khd->hqk', q.astype(jnp.float32), k.astype(jnp.float32)) * HEAD_DIM ** -0.5
    visible = k_pos[None, :] < q_pos[:, None]
    log_beta = jax.nn.log_sigmoid(z)
    log_stay = jnp.where(visible, jax.nn.log_sigmoid(-z), 0.0)
    between = lax.cumsum(log_stay, axis=2, reverse=True) - log_stay
    a = jnp.where(visible, jnp.exp(log_beta + between), 0.0)
    return jnp.einsum('hqk,khd->qhd', a, v.astype(jnp.float32))


def diff_core(q, k, v, q_pos, k_pos, lam, subln_w, out_scale):
    tq, tk = q.shape[0], k.shape[0]
    qc = q.astype(jnp.float32).reshape(tq, DIFF_HEADS, 2, HEAD_DIM)
    kc = k.astype(jnp.float32).reshape(tk, DIFF_HEADS, 2, HEAD_DIM)
    s = jnp.einsum('qhcd,khcd->chqk', qc, kc) * HEAD_DIM ** -0.5
    dist = (q_pos[:, None] - k_pos[None, :]).astype(jnp.float32)
    s = s - alibi_slopes()[:, None, None] * dist
    s = jnp.where(dist >= 0, s, NEG_BIG)
    p = jax.nn.softmax(s, axis=-1)
    a = p[0] - lam * p[1]
    o = jnp.einsum('hqk,khe->qhe', a, v.astype(jnp.float32))
    o = o * lax.rsqrt(jnp.mean(o * o, axis=-1, keepdims=True) + RMS_EPS)
    return o * subln_w.astype(jnp.float32) * out_scale


def prompt_mixers(sb_q, sb_k, sb_v, d_q, d_k, d_v, lam, subln_w, out_scale):
    b, s = sb_q.shape[:2]
    k_pos = jnp.arange(s)

    def one_block(i):
        start = i * Q_BLOCK
        q_pos = start + jnp.arange(Q_BLOCK)
        sbq = lax.dynamic_slice_in_dim(sb_q, start, Q_BLOCK, axis=1)
        dqb = lax.dynamic_slice_in_dim(d_q, start, Q_BLOCK, axis=1)
        o_sb = jax.vmap(lambda q, k, v: stick_breaking_core(q, k, v, q_pos, k_pos))(sbq, sb_k, sb_v)
        o_d = jax.vmap(lambda q, k, v: diff_core(q, k, v, q_pos, k_pos, lam, subln_w, out_scale))(dqb, d_k, d_v)
        return o_sb, o_d

    o_sb, o_d = lax.map(one_block, jnp.arange(s // Q_BLOCK))
    o_sb = jnp.swapaxes(o_sb, 0, 1).reshape(b, s, SB_WIDTH)
    o_d = jnp.swapaxes(o_d, 0, 1).reshape(b, s, DIFF_V_WIDTH)
    return jnp.concatenate([o_sb, o_d], axis=-1)


def sample_mixers(layer, sb_q, sb_k, sb_v, d_q, d_k, d_v, cache_sb_k, cache_sb_v,
                  cache_diff_k, cache_diff_v, page_table, lam, subln_w, out_scale):
    b, n_new = sb_q.shape[:2]
    past_len = page_table.shape[1] * cache_sb_k.shape[2]
    q_pos = past_len + jnp.arange(n_new)
    k_pos = jnp.arange(past_len + n_new)

    def gather(pool, pages, new_rows):
        past = pool[layer, pages].reshape((past_len,) + pool.shape[3:])
        return jnp.concatenate([past.astype(new_rows.dtype), new_rows], axis=0)

    def one_sequence(args):
        pages, sbq, sbk, sbv, dqs, dks, dvs = args
        o_sb = stick_breaking_core(sbq, gather(cache_sb_k, pages, sbk), gather(cache_sb_v, pages, sbv), q_pos, k_pos)
        o_d = diff_core(dqs, gather(cache_diff_k, pages, dks), gather(cache_diff_v, pages, dvs),
                        q_pos, k_pos, lam, subln_w, out_scale)
        return o_sb, o_d

    o_sb, o_d = lax.map(one_sequence, (page_table, sb_q, sb_k, sb_v, d_q, d_k, d_v))
    return jnp.concatenate([o_sb.reshape(b, n_new, SB_WIDTH), o_d.reshape(b, n_new, DIFF_V_WIDTH)], axis=-1)


def moe(x, router_w, router_b, w_gate_up, b_gate_up, w_down, b_down):
    logits = x.astype(jnp.float32) @ router_w.astype(jnp.float32) + router_b.astype(jnp.float32)
    top_val, top_idx = lax.top_k(logits, TOP_K)
    top_w = jax.nn.softmax(top_val, axis=-1)
    gates = jnp.einsum('nk,nke->ne', top_w, jax.nn.one_hot(top_idx, N_EXPERTS, dtype=jnp.float32))
    out = jnp.zeros(x.shape, jnp.float32)
    for e in range(N_EXPERTS):
        gu = x @ w_gate_up[e] + b_gate_up[e]
        g = jnp.minimum(gu[:, :D_FF], SWIGLU_LIMIT).astype(jnp.float32)
        u = jnp.clip(gu[:, D_FF:], -SWIGLU_LIMIT, SWIGLU_LIMIT).astype(jnp.float32)
        act = ((u + 1.0) * g * jax.nn.sigmoid(SWIGLU_ALPHA * g)).astype(x.dtype)
        out = out + gates[:, e:e + 1] * (act @ w_down[e] + b_down[e]).astype(jnp.float32)
    return out.astype(x.dtype)


def setup_inputs(seed: int = 0) -> dict:
    key = jax.random.key(seed)
    ks = jax.random.split(key, 32)
    f32 = jnp.float32
    n_pages = PAST_LEN // PAGE_SIZE
    n_used = DEC_BATCH * n_pages
    n_pool = n_used + max(1, n_used // 4)

    def nrm(k, shape, scale=1.0):
        return jax.random.normal(k, shape, f32) * scale

    return {
        "x_prompt": nrm(ks[0], (BATCH, SEQ, D_MODEL)),
        "x_sample": nrm(ks[1], (DEC_BATCH, DEC_SEQ, D_MODEL)),
        "cache_sb_k": nrm(ks[2], (DEPTH, n_pool, PAGE_SIZE, SB_HEADS, HEAD_DIM)),
        "cache_sb_v": nrm(ks[3], (DEPTH, n_pool, PAGE_SIZE, SB_HEADS, HEAD_DIM)),
        "cache_diff_k": nrm(ks[4], (DEPTH, n_pool, PAGE_SIZE, DIFF_HEADS, 2 * HEAD_DIM)),
        "cache_diff_v": nrm(ks[5], (DEPTH, n_pool, PAGE_SIZE, DIFF_HEADS, DIFF_V_DIM)),
        "page_table": jax.random.permutation(ks[6], n_pool)[:n_used].reshape(DEC_BATCH, n_pages).astype(jnp.int32),
        "c_prompt": nrm(ks[7], (BATCH, D_MODEL)),
        "c_sample": nrm(ks[8], (DEC_BATCH, D_MODEL)),
        "ada_w": nrm(ks[9], (DEPTH, D_MODEL, 6 * D_MODEL), 0.5 * D_MODEL ** -0.5),
        "ada_b": nrm(ks[10], (DEPTH, 6 * D_MODEL), 0.02),
        "norm_attn_w": 1.0 + nrm(ks[11], (DEPTH, D_MODEL), 0.02),
        "w_in": nrm(ks[12], (DEPTH, D_MODEL, IN_PROJ_WIDTH), D_MODEL ** -0.5),
        "diff_lambda_q1": nrm(ks[13], (DEPTH, HEAD_DIM), 0.1),
        "diff_lambda_k1": nrm(ks[14], (DEPTH, HEAD_DIM), 0.1),
        "diff_lambda_q2": nrm(ks[15], (DEPTH, HEAD_DIM), 0.1),
        "diff_lambda_k2": nrm(ks[16], (DEPTH, HEAD_DIM), 0.1),
        "diff_subln_w": 1.0 + nrm(ks[17], (DEPTH, DIFF_V_DIM), 0.02),
        "w_out": nrm(ks[18], (DEPTH, MIX_WIDTH, D_MODEL), MIX_WIDTH ** -0.5),
        "norm_ffn_w": 1.0 + nrm(ks[19], (DEPTH, D_MODEL), 0.02),
        "router_w": nrm(ks[20], (DEPTH, D_MODEL, N_EXPERTS), D_MODEL ** -0.5),
        "router_b": nrm(ks[21], (DEPTH, N_EXPERTS), 0.01),
        "w_gate_up": nrm(ks[22], (DEPTH, N_EXPERTS, D_MODEL, 2 * D_FF), D_MODEL ** -0.5),
        "b_gate_up": nrm(ks[23], (DEPTH, N_EXPERTS, 2 * D_FF), 0.02),
        "w_down": nrm(ks[24], (DEPTH, N_EXPERTS, D_FF, D_MODEL), D_FF ** -0.5),
        "b_down": nrm(ks[25], (DEPTH, N_EXPERTS, D_MODEL), 0.02),
        "final_norm_w": 1.0 + nrm(ks[26], (D_MODEL,), 0.02),
    }


def reference(x_prompt, x_sample, cache_sb_k, cache_sb_v, cache_diff_k, cache_diff_v, page_table,
              c_prompt, c_sample, ada_w, ada_b, norm_attn_w, w_in, diff_lambda_q1, diff_lambda_k1,
              diff_lambda_q2, diff_lambda_k2, diff_subln_w, w_out, norm_ffn_w, router_w, router_b,
              w_gate_up, b_gate_up, w_down, b_down, final_norm_w):
    hp, hs = x_prompt, x_sample
    n_prompt_tok = hp.shape[0] * hp.shape[1]
    p_sb_k, p_sb_v, p_d_k, p_d_v = [], [], [], []
    s_sb_k, s_sb_v, s_d_k, s_d_v = [], [], [], []
    for l in range(DEPTH):
        lambda_init = 0.8 - 0.6 * math.exp(-0.3 * l)
        out_scale = 1.0 - lambda_init
        lam = (jnp.exp(jnp.sum(diff_lambda_q1[l].astype(jnp.float32) * diff_lambda_k1[l].astype(jnp.float32)))
               - jnp.exp(jnp.sum(diff_lambda_q2[l].astype(jnp.float32) * diff_lambda_k2[l].astype(jnp.float32)))
               + lambda_init)
        sh_ap, sc_ap, g_ap, sh_fp, sc_fp, g_fp = ada_params(c_prompt, ada_w[l], ada_b[l])
        sh_as, sc_as, g_as, sh_fs, sc_fs, g_fs = ada_params(c_sample, ada_w[l], ada_b[l])

        up = modulate(rms_norm(hp, norm_attn_w[l]), sh_ap, sc_ap)
        us = modulate(rms_norm(hs, norm_attn_w[l]), sh_as, sc_as)
        sbq_p, sbk_p, sbv_p, dq_p, dk_p, dv_p = in_projection(up, w_in[l])
        sbq_s, sbk_s, sbv_s, dq_s, dk_s, dv_s = in_projection(us, w_in[l])
        mix_p = prompt_mixers(sbq_p, sbk_p, sbv_p, dq_p, dk_p, dv_p, lam, diff_subln_w[l], out_scale)
        mix_s = sample_mixers(l, sbq_s, sbk_s, sbv_s, dq_s, dk_s, dv_s, cache_sb_k, cache_sb_v,
                              cache_diff_k, cache_diff_v, page_table, lam, diff_subln_w[l], out_scale)
        hp = hp + g_ap * (mix_p.astype(hp.dtype) @ w_out[l])
        hs = hs + g_as * (mix_s.astype(hs.dtype) @ w_out[l])
        p_sb_k.append(sbk_p); p_sb_v.append(sbv_p); p_d_k.append(dk_p); p_d_v.append(dv_p)
        s_sb_k.append(sbk_s); s_sb_v.append(sbv_s); s_d_k.append(dk_s); s_d_v.append(dv_s)

        fp = modulate(rms_norm(hp, norm_ffn_w[l]), sh_fp, sc_fp)
        fs = modulate(rms_norm(hs, norm_ffn_w[l]), sh_fs, sc_fs)
        tokens = jnp.concatenate([fp.reshape(-1, D_MODEL), fs.reshape(-1, D_MODEL)], axis=0)
        f_out = moe(tokens, router_w[l], router_b[l], w_gate_up[l], b_gate_up[l], w_down[l], b_down[l])
        hp = hp + g_fp * f_out[:n_prompt_tok].reshape(hp.shape)
        hs = hs + g_fs * f_out[n_prompt_tok:].reshape(hs.shape)

    y_prompt = rms_norm(hp, final_norm_w)
    y_sample = rms_norm(hs, final_norm_w)
    new_sb_k_prompt = jnp.stack(p_sb_k)
    new_sb_v_prompt = jnp.stack(p_sb_v)
    new_diff_k_prompt = jnp.stack(p_d_k)
    new_diff_v_prompt = jnp.stack(p_d_v)
    new_sb_k_sample = jnp.stack(s_sb_k)
    new_sb_v_sample = jnp.stack(s_sb_v)
    new_diff_k_sample = jnp.stack(s_d_k)
    new_diff_v_sample = jnp.stack(s_d_v)
    return (y_prompt, y_sample, new_sb_k_prompt, new_sb_v_prompt, new_diff_k_prompt, new_diff_v_prompt,
            new_sb_k_sample, new_sb_v_sample, new_diff_k_sample, new_diff_v_sample)
```

```python
import functools
import math

import jax
import jax.numpy as jnp
from jax import lax
from jax.experimental import pallas as pl
from jax.experimental.pallas import tpu as pltpu

F32 = jnp.float32
BF16 = jnp.bfloat16
I32 = jnp.int32
U32 = jnp.uint32

HEAD_DIM = 64
SB_HEADS = 8
DIFF_HEADS = 4
DIFF_V_DIM = 2 * HEAD_DIM
SB_WIDTH = SB_HEADS * HEAD_DIM
DIFF_QK_WIDTH = DIFF_HEADS * 2 * HEAD_DIM
DIFF_V_WIDTH = DIFF_HEADS * DIFF_V_DIM
PIECE = 512
N_PIECES = 6
N_EXPERTS = 32
TOP_K = 4
SWIGLU_LIMIT = 7.0
SWIGLU_ALPHA = 1.702
RMS_EPS = 1e-5
NEG_BIG = -1e30
QK_SCALE = HEAD_DIM ** -0.5
SB_SKIP_BELOW = -104.0
ALIBI_SLOPES = tuple(2.0 ** (-8.0 * (h + 1) / DIFF_HEADS) for h in range(DIFF_HEADS))

ATTN_BLOCK = 256
DIFF_BLOCK = 512
TOKEN_BLOCK = 512
PAGES_PER_STEP = 8
MOE_MAX_TOKENS = 3584
MOE_CHUNK = 512
VMEM_LIMIT = 56 * 1024 * 1024


def _cparams(sem, vmem=None):
    return pltpu.CompilerParams(dimension_semantics=sem, vmem_limit_bytes=vmem)


def _dot(a, b):
    return jnp.dot(a, b, preferred_element_type=F32)


def _dot_nt(a, b):
    return lax.dot_general(a, b, (((1,), (1,)), ((), ())), preferred_element_type=F32)


def _split_bf16(x):
    hi = x.astype(BF16)
    lo = (x - hi.astype(F32)).astype(BF16)
    return hi, lo


def _ada_kernel(c_ref, w_ref, b_ref, o_ref):
    c = c_ref[...]
    h = (c * jax.nn.sigmoid(c)).astype(BF16)
    o_ref[...] = _dot(h, w_ref[...].astype(BF16)) + b_ref[...]


def _ada(c, w, b):
    rows, d = c.shape
    n = w.shape[1]
    return pl.pallas_call(
        _ada_kernel,
        grid=(n // d,),
        in_specs=[pl.BlockSpec((rows, d), lambda j: (0, 0)),
                  pl.BlockSpec((d, d), lambda j: (0, j)),
                  pl.BlockSpec((1, d), lambda j: (0, j))],
        out_specs=pl.BlockSpec((rows, d), lambda j: (0, j)),
        out_shape=jax.ShapeDtypeStruct((rows, n), F32),
        compiler_params=_cparams(("arbitrary",)),
        name="ada",
    )(c, w, b.reshape(1, n))


def _norm_mod(x, nw, sh, sc):
    ms = jnp.mean(x * x, axis=-1, keepdims=True)
    y = x * lax.rsqrt(ms + RMS_EPS) * nw
    return y * (1.0 + sc) + sh


def _inproj_prompt_kernel(x_ref, sh_ref, sc_ref, nw_ref, w_ref,
                          sbq_o, sbk_o, sbv_o, dq_o, dk_o, dv_o,
                          sbk_f, sbv_f, dk_f, dv_f):
    u = _norm_mod(x_ref[...], nw_ref[...], sh_ref[0], sc_ref[0]).astype(BF16)
    hm_outs = (sbq_o, sbk_o, sbv_o, dq_o, dk_o, None)
    f_outs = (None, sbk_f, sbv_f, None, dk_f, dv_f)
    scales = (QK_SCALE, None, None, QK_SCALE, None, None)
    for piece in range(N_PIECES):
        p = _dot(u, w_ref[:, piece * PIECE:(piece + 1) * PIECE])
        if f_outs[piece] is not None:
            f_outs[piece][...] = p
        if scales[piece] is not None:
            p = p * scales[piece]
        pb = p.astype(BF16)
        if hm_outs[piece] is not None:
            for h in range(PIECE // HEAD_DIM):
                hm_outs[piece][0, h] = pb[:, h * HEAD_DIM:(h + 1) * HEAD_DIM]
        else:
            for h in range(DIFF_HEADS):
                dv_o[0, h] = pb[:, h * DIFF_V_DIM:(h + 1) * DIFF_V_DIM]


def _inproj_prompt(x, sh, sc, nw, w_bf, batch, seq):
    n, d = x.shape
    tm = min(TOKEN_BLOCK, seq)
    nb = seq // tm
    hm = lambda i: (i // nb, 0, i % nb, 0)
    row = lambda i: (i, 0)
    mod = lambda i: (i // nb, 0, 0)
    hm64 = jax.ShapeDtypeStruct((batch, 8, seq, HEAD_DIM), BF16)
    hm128 = jax.ShapeDtypeStruct((batch, DIFF_HEADS, seq, DIFF_V_DIM), BF16)
    flat = jax.ShapeDtypeStruct((n, PIECE), F32)
    return pl.pallas_call(
        _inproj_prompt_kernel,
        grid=(n // tm,),
        in_specs=[pl.BlockSpec((tm, d), row),
                  pl.BlockSpec((1, 1, d), mod),
                  pl.BlockSpec((1, 1, d), mod),
                  pl.BlockSpec((1, d), lambda i: (0, 0)),
                  pl.BlockSpec((d, N_PIECES * PIECE), lambda i: (0, 0))],
        out_specs=[pl.BlockSpec((1, 8, tm, HEAD_DIM), hm)] * 5
                  + [pl.BlockSpec((1, DIFF_HEADS, tm, DIFF_V_DIM), hm)]
                  + [pl.BlockSpec((tm, PIECE), row)] * 4,
        out_shape=[hm64] * 5 + [hm128] + [flat] * 4,
        compiler_params=_cparams(("arbitrary",), VMEM_LIMIT),
        name="inproj_prompt",
    )(x, sh, sc, nw, w_bf)


def _inproj_sample_kernel(x_ref, sh_ref, sc_ref, nw_ref, w_ref, o_ref):
    u = _norm_mod(x_ref[...], nw_ref[...], sh_ref[...], sc_ref[...]).astype(BF16)
    for piece in range(N_PIECES):
        sl = slice(piece * PIECE, (piece + 1) * PIECE)
        o_ref[:, sl] = _dot(u, w_ref[:, sl])


def _inproj_sample(x, sh, sc, nw, w_bf):
    n, d = x.shape
    tm = min(TOKEN_BLOCK, n)
    row = lambda i: (i, 0)
    return pl.pallas_call(
        _inproj_sample_kernel,
        grid=(n // tm,),
        in_specs=[pl.BlockSpec((tm, d), row), pl.BlockSpec((tm, d), row), pl.BlockSpec((tm, d), row),
                  pl.BlockSpec((1, d), lambda i: (0, 0)),
                  pl.BlockSpec((d, N_PIECES * PIECE), lambda i: (0, 0))],
        out_specs=pl.BlockSpec((tm, N_PIECES * PIECE), row),
        out_shape=jax.ShapeDtypeStruct((n, N_PIECES * PIECE), F32),
        compiler_params=_cparams(("arbitrary",), VMEM_LIMIT),
        name="inproj_sample",
    )(x, sh, sc, nw, w_bf)


def _sb_logs(z):
    t = jnp.log1p(jnp.exp(-jnp.abs(z)))
    return jnp.minimum(z, 0.0) - t, -jnp.maximum(z, 0.0) - t


def _strict_upper(t):
    r = lax.broadcasted_iota(I32, (t, t), 0)
    c = lax.broadcasted_iota(I32, (t, t), 1)
    return jnp.where(r > c, 1.0, 0.0).astype(BF16)


def _suffix_sum(ls, tri):
    hi, lo = _split_bf16(ls)
    return _dot(hi, tri) + _dot(lo, tri)


def _sb_prompt_kernel(q_ref, k_ref, v_ref, o_ref, carry_ref, acc_ref, *, blk):
    i = pl.program_id(1)
    tri = _strict_upper(blk)
    r = lax.broadcasted_iota(I32, (blk, blk), 0)
    c = lax.broadcasted_iota(I32, (blk, blk), 1)
    vis = c < r

    def block(j, diag):
        start = pl.multiple_of(j * blk, blk)
        for h in range(SB_HEADS):
            k = k_ref[0, h, pl.ds(start, blk), :]
            v = v_ref[0, h, pl.ds(start, blk), :]
            lb, ls = _sb_logs(_dot_nt(q_ref[0, h], k))
            if diag:
                ls = jnp.where(vis, ls, 0.0)
            carry = carry_ref[h]
            a = jnp.exp(lb + _suffix_sum(ls, tri) + carry)
            if diag:
                a = jnp.where(vis, a, 0.0)
            acc_ref[h] = acc_ref[h] + _dot(a.astype(BF16), v)
            carry_ref[h] = carry + jnp.sum(ls, axis=1, keepdims=True)

    carry_ref[...] = jnp.zeros_like(carry_ref)
    acc_ref[...] = jnp.zeros_like(acc_ref)
    block(i, True)

    def cond(s):
        return jnp.logical_and(s[0] >= 0, s[1] > SB_SKIP_BELOW)

    def body(s):
        block(s[0], False)
        return s[0] - 1, jnp.max(carry_ref[...])

    lax.while_loop(cond, body, (i - 1, jnp.max(carry_ref[...])))
    o_ref[0] = jnp.concatenate([acc_ref[h] for h in range(SB_HEADS)], axis=1).astype(o_ref.dtype)


def _sb_prompt(q, k, v):
    b, h, s, dh = q.shape
    blk = min(ATTN_BLOCK, s)
    return pl.pallas_call(
        functools.partial(_sb_prompt_kernel, blk=blk),
        grid=(b, s // blk),
        in_specs=[pl.BlockSpec((1, h, blk, dh), lambda bi, i: (bi, 0, i, 0)),
                  pl.BlockSpec((1, h, s, dh), lambda bi, i: (bi, 0, 0, 0)),
                  pl.BlockSpec((1, h, s, dh), lambda bi, i: (bi, 0, 0, 0))],
        out_specs=pl.BlockSpec((1, blk, h * dh), lambda bi, i: (bi, i, 0)),
        out_shape=jax.ShapeDtypeStruct((b, s, h * dh), BF16),
        scratch_shapes=[pltpu.VMEM((h, blk, 1), F32), pltpu.VMEM((h, blk, dh), F32)],
        compiler_params=_cparams(("arbitrary", "arbitrary"), VMEM_LIMIT),
        name="sb_prompt",
    )(q, k, v)


def _lambda_from(lamv_ref, lambda_init):
    l1 = jnp.sum(lamv_ref[0:1, :] * lamv_ref[1:2, :], axis=1, keepdims=True)
    l2 = jnp.sum(lamv_ref[2:3, :] * lamv_ref[3:4, :], axis=1, keepdims=True)
    return jnp.exp(l1) - jnp.exp(l2) + lambda_init


def _head_slope(h):
    out = jnp.float32(ALIBI_SLOPES[-1])
    for hh in range(DIFF_HEADS - 2, -1, -1):
        out = jnp.where(h == hh, jnp.float32(ALIBI_SLOPES[hh]), out)
    return out


def _diff_prompt_kernel(lamv_ref, sw_ref, q_ref, k_ref, v_ref, o_ref, *, blk, lambda_init):
    h = pl.program_id(1)
    i = pl.program_id(2)
    slope = _head_slope(h)
    r = lax.broadcasted_iota(I32, (blk, blk), 0)
    c = lax.broadcasted_iota(I32, (blk, blk), 1)
    srel = slope * (r - c).astype(F32)
    causal = c <= r
    qs = (q_ref[0, 0], q_ref[0, 1])

    def block(j, st, diag):
        start = pl.multiple_of(j * blk, blk)
        v = v_ref[0, 0, pl.ds(start, blk), :]
        off = slope * ((i - j) * blk).astype(F32)
        new = []
        for comp in range(2):
            m, l, acc = st[3 * comp:3 * comp + 3]
            k = k_ref[0, comp, pl.ds(start, blk), :]
            s = _dot_nt(qs[comp], k) - srel
            if diag:
                s = jnp.where(causal, s, NEG_BIG)
            m_new = jnp.maximum(m, jnp.max(s, axis=1, keepdims=True) - off)
            p = jnp.exp(s - (m_new + off))
            alpha = jnp.exp(m - m_new)
            l = l * alpha + jnp.sum(p, axis=1, keepdims=True)
            acc = acc * alpha + _dot(p.astype(BF16), v)
            new += [m_new, l, acc]
        return tuple(new)

    init = (jnp.full((blk, 1), NEG_BIG, F32), jnp.zeros((blk, 1), F32),
            jnp.zeros((blk, DIFF_V_DIM), F32)) * 2
    st = lax.fori_loop(0, i, lambda j, st: block(j, st, False), init)
    st = block(i, st, True)
    lam = _lambda_from(lamv_ref, lambda_init)
    o = st[2] / st[1] - lam * (st[5] / st[4])
    o = o * lax.rsqrt(jnp.mean(o * o, axis=1, keepdims=True) + RMS_EPS)
    o_ref[0] = (o * sw_ref[...] * (1.0 - lambda_init)).astype(o_ref.dtype)


def _diff_prompt(lamv, sw, q, k, v, lambda_init):
    b, h2, s, dh = q.shape
    blk = min(DIFF_BLOCK, s)
    return pl.pallas_call(
        functools.partial(_diff_prompt_kernel, blk=blk, lambda_init=lambda_init),
        grid=(b, DIFF_HEADS, s // blk),
        in_specs=[pl.BlockSpec((4, HEAD_DIM), lambda bi, hi, i: (0, 0)),
                  pl.BlockSpec((1, DIFF_V_DIM), lambda bi, hi, i: (0, 0)),
                  pl.BlockSpec((1, 2, blk, dh), lambda bi, hi, i: (bi, hi, i, 0)),
                  pl.BlockSpec((1, 2, s, dh), lambda bi, hi, i: (bi, hi, 0, 0)),
                  pl.BlockSpec((1, 1, s, DIFF_V_DIM), lambda bi, hi, i: (bi, hi, 0, 0))],
        out_specs=pl.BlockSpec((1, blk, DIFF_V_DIM), lambda bi, hi, i: (bi, i, hi)),
        out_shape=jax.ShapeDtypeStruct((b, s, DIFF_V_WIDTH), BF16),
        compiler_params=_cparams(("arbitrary", "arbitrary", "arbitrary"), VMEM_LIMIT),
        name="diff_prompt",
    )(lamv, sw, q, k, v)


def _sample_kernel(pt_ref, lamv_ref, sw_ref, qsb_ref, qd_ref, new_ref, *rest,
                   n_new, page, n_pages, pages_per_step, lambda_init):
    g_ = pages_per_step
    sbk_pages = rest[0 * g_:1 * g_]
    sbv_pages = rest[1 * g_:2 * g_]
    dk_pages = rest[2 * g_:3 * g_]
    dv_pages = rest[3 * g_:4 * g_]
    o_ref = rest[4 * g_]
    sb_carry, sb_acc, d_m, d_l, d_acc = rest[4 * g_ + 1:]
    t = pl.program_id(1)
    rows = 8 * n_new
    hrows = 2 * n_new
    past_len = n_pages * page

    ridx = lax.broadcasted_iota(I32, (rows, 1), 0)
    qidx = ridx % n_new
    grp = ridx // n_new
    slope_row = jnp.zeros((rows, 1), F32)
    for hh in range(DIFF_HEADS):
        slope_row = jnp.where(grp // 2 == hh, jnp.float32(ALIBI_SLOPES[hh]), slope_row)
    qpos = (past_len + qidx).astype(F32)

    @pl.when(t == 0)
    def _():
        qsb = qsb_ref[0].astype(F32)
        col = lax.broadcasted_iota(I32, (rows, DIFF_QK_WIDTH), 1)
        qd = jnp.where(col // DIFF_V_DIM == grp // 2,
                       jnp.concatenate([qd_ref[0].astype(F32)] * DIFF_HEADS, axis=1), 0.0)
        carry = jnp.zeros((rows, 1), F32)
        acc = jnp.zeros((rows, SB_WIDTH), F32)
        m = jnp.full((rows, 1), NEG_BIG, F32)
        l = jnp.zeros((rows, 1), F32)
        dacc = jnp.zeros((rows, DIFF_V_DIM), F32)
        for j in range(n_new - 1, -1, -1):
            sbk = new_ref[0, j:j + 1, 1 * PIECE:2 * PIECE]
            sbv = new_ref[0, j:j + 1, 2 * PIECE:3 * PIECE]
            dk = new_ref[0, j:j + 1, 4 * PIECE:5 * PIECE]
            dv = new_ref[0, j:j + 1, 5 * PIECE:6 * PIECE]
            z = jnp.sum(qsb * sbk, axis=1, keepdims=True)
            lb, ls = _sb_logs(z)
            vis = j < qidx
            ls = jnp.where(vis, ls, 0.0)
            a = jnp.where(vis, jnp.exp(lb + carry), 0.0)
            acc = acc + a * sbv
            carry = carry + ls
            s = jnp.sum(qd * dk, axis=1, keepdims=True) - slope_row * (qidx - j).astype(F32)
            s = jnp.where(j <= qidx, s, NEG_BIG)
            m_new = jnp.maximum(m, s)
            p = jnp.exp(s - m_new)
            alpha = jnp.exp(m - m_new)
            l = l * alpha + p
            dv_rows = jnp.concatenate(
                [jnp.broadcast_to(dv[:, hh * DIFF_V_DIM:(hh + 1) * DIFF_V_DIM], (hrows, DIFF_V_DIM))
                 for hh in range(DIFF_HEADS)], axis=0)
            dacc = dacc * alpha + p * dv_rows
            m = m_new
        sb_carry[...] = carry
        sb_acc[...] = acc
        d_m[...] = m
        d_l[...] = l
        d_acc[...] = dacc

    @pl.when(jnp.max(sb_carry[...]) > SB_SKIP_BELOW)
    def _():
        qsb = qsb_ref[0]
        tri = _strict_upper(page)
        z = jnp.concatenate([_dot(qsb, sbk_pages[g][0].astype(BF16)) for g in range(g_)], axis=0)
        lb, ls = _sb_logs(z)
        suffix = _suffix_sum(ls, tri)
        rowsum = jnp.sum(ls, axis=1, keepdims=True)
        carry = sb_carry[...]
        carries = []
        for g in range(g_):
            carries.append(carry)
            carry = carry + rowsum[g * rows:(g + 1) * rows]
        a = jnp.exp(lb + suffix + jnp.concatenate(carries, axis=0)).astype(BF16)
        acc = sb_acc[...]
        for g in range(g_):
            acc = acc + _dot_nt(a[g * rows:(g + 1) * rows], sbv_pages[g][0].astype(BF16))
        sb_acc[...] = acc
        sb_carry[...] = carry

    qd = qd_ref[0]
    lane = lax.broadcasted_iota(I32, (1, page), 1).astype(F32)
    ss, bases = [], []
    for g in range(g_):
        pg = n_pages - 1 - (t * g_ + g)
        for hh in range(DIFF_HEADS):
            kh = dk_pages[g][0, pl.ds(hh, page, stride=DIFF_HEADS), :].astype(BF16)
            ss.append(_dot_nt(qd[hh * hrows:(hh + 1) * hrows], kh))
        bases.append(qpos - (pg * page).astype(F32))
    s = jnp.concatenate(ss, axis=0)
    base = jnp.concatenate(bases, axis=0)
    slope_all = jnp.concatenate([slope_row] * g_, axis=0)
    s = s - slope_all * (base - lane)
    rowmax = jnp.max(s, axis=1, keepdims=True)
    m_old = d_m[...]
    m_new = m_old
    for g in range(g_):
        m_new = jnp.maximum(m_new, rowmax[g * rows:(g + 1) * rows])
    p = jnp.exp(s - jnp.concatenate([m_new] * g_, axis=0))
    rowsum = jnp.sum(p, axis=1, keepdims=True)
    alpha = jnp.exp(m_old - m_new)
    l = d_l[...] * alpha
    for g in range(g_):
        l = l + rowsum[g * rows:(g + 1) * rows]
    pb = p.astype(BF16)
    pvs = []
    for hh in range(DIFF_HEADS):
        pv = jnp.zeros((hrows, DIFF_V_DIM), F32)
        for g in range(g_):
            vh = dv_pages[g][0, pl.ds(hh, page, stride=DIFF_HEADS), :].astype(BF16)
            pv = pv + _dot(pb[g * rows + hh * hrows:g * rows + (hh + 1) * hrows], vh)
        pvs.append(pv)
    d_acc[...] = d_acc[...] * alpha + jnp.concatenate(pvs, axis=0)
    d_l[...] = l
    d_m[...] = m_new

    @pl.when(t == pl.num_programs(1) - 1)
    def _():
        col = lax.broadcasted_iota(I32, (rows, SB_WIDTH), 1)
        qrow = lax.broadcasted_iota(I32, (8, rows), 0)
        rcol = lax.broadcasted_iota(I32, (8, rows), 1)
        sel = jnp.where(rcol % n_new == qrow, 1.0, 0.0).astype(BF16)
        hi, lo = _split_bf16(jnp.where(col // HEAD_DIM == grp, sb_acc[...], 0.0))
        o_sb = _dot(sel, hi) + _dot(sel, lo)

        lam = _lambda_from(lamv_ref, lambda_init)
        on = d_acc[...] / d_l[...]
        segs = []
        for hh in range(DIFF_HEADS):
            both = on[hh * hrows:(hh + 1) * hrows]
            seg = both - lam * pltpu.roll(both, n_new, axis=0)
            seg = seg * lax.rsqrt(jnp.mean(seg * seg, axis=1, keepdims=True) + RMS_EPS)
            segs.append(seg * sw_ref[...] * (1.0 - lambda_init))
        full = jnp.concatenate([o_sb] + segs, axis=1)
        o_ref[0] = full[:n_new].astype(o_ref.dtype)


def _sample_attention(page_table, lamv, sw, qsb_bd, qd_bd, p_new, caches, lambda_init):
    nseq, n_pages = page_table.shape
    n_new = p_new.shape[1]
    assert 2 * n_new == 8, "differential rows per head must fill one sublane group"
    rows_per_page = caches[0].shape[1]
    page = caches[0].shape[2]
    g_ = math.gcd(PAGES_PER_STEP, n_pages)
    rows = 8 * n_new

    def page_spec(g):
        return pl.BlockSpec((1, rows_per_page, page),
                            lambda b, t, pt, g=g: (pt[b, n_pages - 1 - (t * g_ + g)], 0, 0))

    cache_specs, cache_args = [], []
    for cache in caches:
        for g in range(g_):
            cache_specs.append(page_spec(g))
            cache_args.append(cache)
    grid_spec = pltpu.PrefetchScalarGridSpec(
        num_scalar_prefetch=1,
        grid=(nseq, n_pages // g_),
        in_specs=[pl.BlockSpec((4, HEAD_DIM), lambda b, t, pt: (0, 0)),
                  pl.BlockSpec((1, DIFF_V_DIM), lambda b, t, pt: (0, 0)),
                  pl.BlockSpec((1, rows, SB_WIDTH), lambda b, t, pt: (b, 0, 0)),
                  pl.BlockSpec((1, rows, DIFF_V_DIM), lambda b, t, pt: (b, 0, 0)),
                  pl.BlockSpec((1, n_new, N_PIECES * PIECE), lambda b, t, pt: (b, 0, 0))] + cache_specs,
        out_specs=pl.BlockSpec((1, n_new, 2 * PIECE), lambda b, t, pt: (b, 0, 0)),
        scratch_shapes=[pltpu.VMEM((rows, 1), F32), pltpu.VMEM((rows, SB_WIDTH), F32),
                        pltpu.VMEM((rows, 1), F32), pltpu.VMEM((rows, 1), F32),
                        pltpu.VMEM((rows, DIFF_V_DIM), F32)],
    )
    return pl.pallas_call(
        functools.partial(_sample_kernel, n_new=n_new, page=page, n_pages=n_pages,
                          pages_per_step=g_, lambda_init=lambda_init),
        grid_spec=grid_spec,
        out_shape=jax.ShapeDtypeStruct((nseq, n_new, 2 * PIECE), BF16),
        compiler_params=_cparams(("arbitrary", "arbitrary"), VMEM_LIMIT),
        name="sample_attention",
    )(page_table, lamv, sw, qsb_bd, qd_bd, p_new, *cache_args)


def _sb_query_rows(q):
    nseq, n_new, w = q.shape
    tiled = jnp.broadcast_to(q[:, None], (nseq, 8, n_new, w)).reshape(nseq, 8 * n_new, w)
    grp = (jnp.arange(8 * n_new) // n_new)[:, None]
    col = (jnp.arange(w) // HEAD_DIM)[None, :]
    return jnp.where(grp == col, tiled * QK_SCALE, 0.0).astype(BF16)


def _diff_query_rows(q):
    nseq, n_new, _ = q.shape
    q5 = q.reshape(nseq, n_new, DIFF_HEADS, 2, HEAD_DIM) * QK_SCALE
    q5 = jnp.transpose(q5, (0, 2, 3, 1, 4))
    eye = jnp.eye(2, dtype=q.dtype)[None, None, :, None, :, None]
    out = q5[:, :, :, :, None, :] * eye
    return out.reshape(nseq, 8 * n_new, DIFF_V_DIM).astype(BF16)


def _post_kernel(x_ref, osb_ref, od_ref, g_ref, sh_ref, sc_ref, nw_ref, wo_ref, rw_hi_ref, rw_lo_ref,
                 rb_ref, h_ref, f_ref, gates_ref, *, per_token):
    take = (lambda r: r[...]) if per_token else (lambda r: r[0])
    proj = _dot(osb_ref[...], wo_ref[:SB_WIDTH, :]) + _dot(od_ref[...], wo_ref[SB_WIDTH:, :])
    h = x_ref[...] + take(g_ref) * proj
    h_ref[...] = h
    f = _norm_mod(h, nw_ref[...], take(sh_ref), take(sc_ref))

    f_ref[...] = f

    f_hi, f_lo = _split_bf16(f)
    logits = (_dot(f_hi, rw_hi_ref[...]) + _dot(f_lo, rw_hi_ref[...]) + _dot(f_hi, rw_lo_ref[...])
              + rb_ref[...])
    lane = lax.broadcasted_iota(I32, logits.shape, 1).astype(F32)
    work = logits
    picks, vals = [], []
    for _ in range(TOP_K):
        mx = jnp.max(work, axis=1, keepdims=True)
        first = jnp.min(jnp.where(work == mx, lane, float(N_EXPERTS)), axis=1, keepdims=True)
        pick = lane == first
        picks.append(pick)
        vals.append(mx)
        work = jnp.where(pick, -jnp.inf, work)
    es = [jnp.exp(v - vals[0]) for v in vals]
    denom = es[0] + es[1] + es[2] + es[3]
    gates = jnp.zeros_like(logits)
    for pick, e in zip(picks, es):
        gates = jnp.where(pick, e / denom, gates)
    gates_ref[...] = gates


def _post(x, osb, od, g, sh, sc, nw, wo_bf, rw_hi, rw_lo, rb, tokens_per_group):
    n, d = x.shape
    per_token = tokens_per_group is None
    tm = min(TOKEN_BLOCK, n if per_token else tokens_per_group)
    row = lambda i: (i, 0)
    const = lambda i: (0, 0)
    if per_token:
        mod_spec = pl.BlockSpec((tm, d), row)
    else:
        nb = tokens_per_group // tm
        mod_spec = pl.BlockSpec((1, 1, d), lambda i: (i // nb, 0, 0))
    return pl.pallas_call(
        functools.partial(_post_kernel, per_token=per_token),
        grid=(n // tm,),
        in_specs=[pl.BlockSpec((tm, d), row),
                  pl.BlockSpec((tm, SB_WIDTH), row),
                  pl.BlockSpec((tm, DIFF_V_WIDTH), row),
                  mod_spec, mod_spec, mod_spec,
                  pl.BlockSpec((1, d), const),
                  pl.BlockSpec((SB_WIDTH + DIFF_V_WIDTH, d), const),
                  pl.BlockSpec((d, N_EXPERTS), const),
                  pl.BlockSpec((d, N_EXPERTS), const),
                  pl.BlockSpec((1, N_EXPERTS), const)],
        out_specs=[pl.BlockSpec((tm, d), row),
                   pl.BlockSpec((tm, d), row),
                   pl.BlockSpec((tm, N_EXPERTS), row)],
        out_shape=[jax.ShapeDtypeStruct((n, d), F32),
                   jax.ShapeDtypeStruct((n, d), F32),
                   jax.ShapeDtypeStruct((n, N_EXPERTS), F32)],
        compiler_params=_cparams(("arbitrary",), VMEM_LIMIT),
        name="post_attention",
    )(x, osb, od, g, sh, sc, nw, wo_bf, rw_hi, rw_lo, rb)


def _moe_plan(n_tokens):
    nblk = 1
    while n_tokens % nblk or n_tokens // nblk > MOE_MAX_TOKENS or (n_tokens // nblk) % 8:
        nblk += 1
    ttok = n_tokens // nblk
    mc = min(MOE_CHUNK, ttok)
    smax = N_EXPERTS + (ttok * TOP_K) // mc
    return nblk, ttok, mc, smax


def _routing(gates, nblk, ttok, mc, smax):
    sel = (gates > 0).reshape(nblk, ttok, N_EXPERTS)
    cnt = jnp.sum(sel.astype(I32), axis=1)
    nch = (cnt + mc - 1) // mc
    ends = jnp.cumsum(nch, axis=1)
    starts = ends - nch
    total = ends[:, -1]
    s_ar = jnp.arange(smax, dtype=I32)
    slot_e = jnp.sum((s_ar[None, :, None] >= ends[:, None, :]).astype(I32), axis=-1)
    valid = s_ar[None, :] < total[:, None]
    last_e = jnp.take_along_axis(slot_e, jnp.maximum(total - 1, 0)[:, None], axis=1)
    slot_e = jnp.where(valid, jnp.minimum(slot_e, N_EXPERTS - 1), jnp.minimum(last_e, N_EXPERTS - 1))
    chunk = s_ar[None, :] - jnp.take_along_axis(starts, slot_e, axis=1)
    slot_cnt = jnp.where(valid, jnp.clip(jnp.take_along_axis(cnt, slot_e, axis=1) - chunk * mc, 0, mc), 0)
    order = jnp.argsort(jnp.logical_not(jnp.swapaxes(sel, 1, 2)), axis=2, stable=True).astype(I32)
    lane = jnp.arange(mc, dtype=I32)
    pos = jnp.minimum(chunk[:, :, None] * mc + lane[None, None, :], ttok - 1)
    flat = (jnp.arange(nblk, dtype=I32)[:, None, None] * N_EXPERTS + slot_e[:, :, None]) * ttok + pos
    tok = jnp.take(order.reshape(-1), flat.reshape(-1)).reshape(nblk, smax, mc)
    live = lane[None, None, :] < slot_cnt[:, :, None]
    tok = jnp.where(live, tok, 0)
    gidx = (jnp.arange(nblk, dtype=I32)[:, None, None] * ttok + tok) * N_EXPERTS + slot_e[:, :, None]
    gw = jnp.where(live, jnp.take(gates.reshape(-1), gidx.reshape(-1)).reshape(nblk, smax, mc), 0.0)
    return slot_e.reshape(-1), slot_cnt.reshape(-1).astype(I32), tok, gw


def _moe_kernel(se_ref, sc_ref, x_hbm, idx_ref, gw_ref, wgu_ref, bgu_ref, wd_ref, bd_ref, out_hbm,
                x_ref, acc_ref, xg_ref, y_ref, sem, *, ttok, mc, smax, d_ff):
    tb = pl.program_id(0)
    s = pl.program_id(1)
    cnt = sc_ref[tb * smax + s]

    @pl.when(s == 0)
    def _():
        cp = pltpu.make_async_copy(x_hbm.at[pl.ds(tb * ttok, ttok), :], x_ref, sem.at[0])
        cp.start()
        acc_ref[...] = jnp.zeros_like(acc_ref)
        cp.wait()

    @pl.when(cnt > 0)
    def _():
        ngrp = (cnt + 7) // 8

        def gather(gi, carry):
            for u in range(8):
                r = gi * 8 + u
                xg_ref[pl.ds(r, 1), :] = x_ref[pl.ds(idx_ref[0, 0, r], 1), :]
            return carry

        lax.fori_loop(0, mc // 8, gather, 0)
        gu = _dot(xg_ref[...].astype(BF16), wgu_ref[0]) + bgu_ref[0]
        g = jnp.minimum(gu[:, :d_ff], SWIGLU_LIMIT)
        u = jnp.clip(gu[:, d_ff:], -SWIGLU_LIMIT, SWIGLU_LIMIT)
        act = ((u + 1.0) * g * jax.nn.sigmoid(SWIGLU_ALPHA * g)).astype(BF16)
        y_ref[...] = _dot(act, wd_ref[0]) + bd_ref[0]

        def scatter(gi, carry):
            for u_ in range(8):
                r = gi * 8 + u_
                tok = idx_ref[0, 0, r]
                acc_ref[pl.ds(tok, 1), :] = acc_ref[pl.ds(tok, 1), :] + gw_ref[0, 0, r] * y_ref[pl.ds(r, 1), :]
            return carry

        lax.fori_loop(0, ngrp, scatter, 0)

    @pl.when(s == smax - 1)
    def _():
        cp = pltpu.make_async_copy(acc_ref, out_hbm.at[pl.ds(tb * ttok, ttok), :], sem.at[1])
        cp.start()
        cp.wait()


def _moe(x, gates, wgu_bf, bgu, wd_bf, bd):
    n, d = x.shape
    d_ff = wd_bf.shape[1]
    nblk, ttok, mc, smax = _moe_plan(n)
    slot_e, slot_cnt, tok, gw = _routing(gates, nblk, ttok, mc, smax)
    slot = lambda tb, s, se, sc: (tb * smax + s, 0, 0)
    expert = lambda tb, s, se, sc: (se[tb * smax + s], 0, 0)
    grid_spec = pltpu.PrefetchScalarGridSpec(
        num_scalar_prefetch=2,
        grid=(nblk, smax),
        in_specs=[pl.BlockSpec(memory_space=pl.ANY),
                  pl.BlockSpec((1, 1, mc), slot, memory_space=pltpu.SMEM),
                  pl.BlockSpec((1, 1, mc), slot, memory_space=pltpu.SMEM),
                  pl.BlockSpec((1, d, 2 * d_ff), expert),
                  pl.BlockSpec((1, 1, 2 * d_ff), expert),
                  pl.BlockSpec((1, d_ff, d), expert),
                  pl.BlockSpec((1, 1, d), expert)],
        out_specs=pl.BlockSpec(memory_space=pl.ANY),
        scratch_shapes=[pltpu.VMEM((ttok, d), F32), pltpu.VMEM((ttok, d), F32), pltpu.VMEM((mc, d), F32),
                        pltpu.VMEM((mc, d), F32), pltpu.SemaphoreType.DMA((2,))],
    )
    return pl.pallas_call(
        functools.partial(_moe_kernel, ttok=ttok, mc=mc, smax=smax, d_ff=d_ff),
        grid_spec=grid_spec,
        out_shape=jax.ShapeDtypeStruct((n, d), F32),
        compiler_params=_cparams(("arbitrary", "arbitrary"), VMEM_LIMIT),
        name="moe",
    )(slot_e, slot_cnt, x, tok.reshape(nblk * smax, 1, mc), gw.reshape(nblk * smax, 1, mc),
      wgu_bf, bgu, wd_bf, bd)


def _final_kernel(h_ref, f_ref, g_ref, nw_ref, o_ref, *, per_token, last):
    g = g_ref[...] if per_token else g_ref[0]
    h = h_ref[...] + g * f_ref[...]
    if last:
        h = h * lax.rsqrt(jnp.mean(h * h, axis=-1, keepdims=True) + RMS_EPS) * nw_ref[...]
    o_ref[...] = h


def _final(h, f_all, row0, g, nw, tokens_per_group, last):
    n, d = h.shape
    per_token = tokens_per_group is None
    tm = min(TOKEN_BLOCK, n if per_token else tokens_per_group)
    assert row0 % tm == 0
    b0 = row0 // tm
    row = lambda i: (i, 0)
    if per_token:
        g_spec = pl.BlockSpec((tm, d), row)
    else:
        nb = tokens_per_group // tm
        g_spec = pl.BlockSpec((1, 1, d), lambda i: (i // nb, 0, 0))
    return pl.pallas_call(
        functools.partial(_final_kernel, per_token=per_token, last=last),
        grid=(n // tm,),
        in_specs=[pl.BlockSpec((tm, d), row),
                  pl.BlockSpec((tm, d), lambda i: (b0 + i, 0)),
                  g_spec,
                  pl.BlockSpec((1, d), lambda i: (0, 0))],
        out_specs=pl.BlockSpec((tm, d), row),
        out_shape=jax.ShapeDtypeStruct((n, d), F32),
        compiler_params=_cparams(("arbitrary",), VMEM_LIMIT),
        name="final",
    )(h, f_all, g, nw)


def kernel(x_prompt, x_sample, cache_sb_k, cache_sb_v, cache_diff_k, cache_diff_v, page_table, c_prompt, c_sample, ada_w, ada_b, norm_attn_w, w_in, diff_lambda_q1, diff_lambda_k1, diff_lambda_q2, diff_lambda_k2, diff_subln_w, w_out, norm_ffn_w, router_w, router_b, w_gate_up, b_gate_up, w_down, b_down, final_norm_w):
    batch, seq, d = x_prompt.shape
    nseq, n_new, _ = x_sample.shape
    depth = ada_w.shape[0]
    n_p, n_s = batch * seq, nseq * n_new
    n_pool, page = cache_sb_k.shape[1], cache_sb_k.shape[2]

    hp = x_prompt.reshape(n_p, d)
    hs = x_sample.reshape(n_s, d)
    c_all = jnp.concatenate([c_prompt, c_sample], axis=0)
    new_kv = [[] for _ in range(8)]
    final_w = final_norm_w.reshape(1, d)

    for l in range(depth):
        lambda_init = 0.8 - 0.6 * math.exp(-0.3 * l)
        ada = _ada(c_all, ada_w[l], ada_b[l])
        mods_p = [ada[:batch, i * d:(i + 1) * d].reshape(batch, 1, d) for i in range(6)]
        mods_s = [jnp.repeat(ada[batch:, i * d:(i + 1) * d], n_new, axis=0) for i in range(6)]
        lamv = jnp.stack([diff_lambda_q1[l], diff_lambda_k1[l], diff_lambda_q2[l], diff_lambda_k2[l]]).astype(F32)
        sw = diff_subln_w[l].reshape(1, DIFF_V_DIM).astype(F32)
        nw_a = norm_attn_w[l].reshape(1, d)
        nw_f = norm_ffn_w[l].reshape(1, d)
        w_in_bf = w_in[l].astype(BF16)
        w_out_bf = w_out[l].astype(BF16)
        rw_hi = router_w[l].astype(BF16)
        rw_lo = (router_w[l] - rw_hi.astype(F32)).astype(BF16)
        rb = router_b[l].reshape(1, N_EXPERTS)

        (sbq, sbk, sbv, dq, dk, dv, sbk_f, sbv_f, dk_f, dv_f) = _inproj_prompt(
            hp, mods_p[0], mods_p[1], nw_a, w_in_bf, batch, seq)
        o_sb = _sb_prompt(sbq, sbk, sbv).reshape(n_p, SB_WIDTH)
        o_d = _diff_prompt(lamv, sw, dq, dk, dv, lambda_init).reshape(n_p, DIFF_V_WIDTH)

        p_s = _inproj_sample(hs, mods_s[0], mods_s[1], nw_a, w_in_bf).reshape(nseq, n_new, N_PIECES * PIECE)
        caches = [jnp.transpose(c[l], (0, 2, 3, 1)).reshape(n_pool, SB_WIDTH, page)
                  for c in (cache_sb_k, cache_sb_v)]
        caches += [c[l].reshape(n_pool, page * DIFF_HEADS, DIFF_V_DIM) for c in (cache_diff_k, cache_diff_v)]
        o_s = _sample_attention(page_table, lamv, sw,
                                _sb_query_rows(p_s[:, :, 0:PIECE]),
                                _diff_query_rows(p_s[:, :, 3 * PIECE:4 * PIECE]),
                                p_s, caches, lambda_init).reshape(n_s, 2 * PIECE)

        h1_p, f_p, gates_p = _post(hp, o_sb, o_d, mods_p[2], mods_p[3], mods_p[4], nw_f, w_out_bf,
                                   rw_hi, rw_lo, rb, seq)
        h1_s, f_s, gates_s = _post(hs, o_s[:, :SB_WIDTH], o_s[:, SB_WIDTH:], mods_s[2], mods_s[3], mods_s[4],
                                   nw_f, w_out_bf, rw_hi, rw_lo, rb, None)

        f_out = _moe(jnp.concatenate([f_p, f_s], axis=0), jnp.concatenate([gates_p, gates_s], axis=0),
                     w_gate_up[l].astype(BF16), b_gate_up[l].reshape(N_EXPERTS, 1, -1),
                     w_down[l].astype(BF16), b_down[l].reshape(N_EXPERTS, 1, -1))
        last = l == depth - 1
        hp = _final(h1_p, f_out, 0, mods_p[5], final_w, seq, last)
        hs = _final(h1_s, f_out, n_p, mods_s[5], final_w, None, last)

        p_s_flat = p_s.reshape(n_s, N_PIECES * PIECE)
        new_kv[0].append(sbk_f.reshape(batch, seq, SB_HEADS, HEAD_DIM))
        new_kv[1].append(sbv_f.reshape(batch, seq, SB_HEADS, HEAD_DIM))
        new_kv[2].append(dk_f.reshape(batch, seq, DIFF_HEADS, 2 * HEAD_DIM))
        new_kv[3].append(dv_f.reshape(batch, seq, DIFF_HEADS, DIFF_V_DIM))
        new_kv[4].append(p_s_flat[:, 1 * PIECE:2 * PIECE].reshape(nseq, n_new, SB_HEADS, HEAD_DIM))
        new_kv[5].append(p_s_flat[:, 2 * PIECE:3 * PIECE].reshape(nseq, n_new, SB_HEADS, HEAD_DIM))
        new_kv[6].append(p_s_flat[:, 4 * PIECE:5 * PIECE].reshape(nseq, n_new, DIFF_HEADS, 2 * HEAD_DIM))
        new_kv[7].append(p_s_flat[:, 5 * PIECE:6 * PIECE].reshape(nseq, n_new, DIFF_HEADS, DIFF_V_DIM))

    return (hp.reshape(batch, seq, d), hs.reshape(nseq, n_new, d)) + tuple(jnp.stack(v) for v in new_kv)
```

```python
import functools
import math

import jax
import jax.numpy as jnp
from jax import lax
from jax.experimental import pallas as pl
from jax.experimental.pallas import tpu as pltpu

F32 = jnp.float32
BF16 = jnp.bfloat16
I32 = jnp.int32
U32 = jnp.uint32

HEAD_DIM = 64
SB_HEADS = 8
DIFF_HEADS = 4
DIFF_V_DIM = 2 * HEAD_DIM
SB_WIDTH = SB_HEADS * HEAD_DIM
DIFF_QK_WIDTH = DIFF_HEADS * 2 * HEAD_DIM
DIFF_V_WIDTH = DIFF_HEADS * DIFF_V_DIM
PIECE = 512
N_PIECES = 6
N_EXPERTS = 32
TOP_K = 4
SWIGLU_LIMIT = 7.0
SWIGLU_ALPHA = 1.702
RMS_EPS = 1e-5
NEG_BIG = -1e30
QK_SCALE = HEAD_DIM ** -0.5
SB_SKIP_BELOW = -104.0
ALIBI_SLOPES = tuple(2.0 ** (-8.0 * (h + 1) / DIFF_HEADS) for h in range(DIFF_HEADS))

ATTN_BLOCK = 256
DIFF_BLOCK = 512
TOKEN_BLOCK = 512
PAGES_PER_STEP = 8
MOE_MAX_TOKENS = 3584
MOE_CHUNK = 512
MOE_PAD_ROWS = 8
VMEM_LIMIT = 56 * 1024 * 1024


def _cparams(sem, vmem=None):
    return pltpu.CompilerParams(dimension_semantics=sem, vmem_limit_bytes=vmem)


def _dot(a, b):
    return jnp.dot(a, b, preferred_element_type=F32)


def _dot_nt(a, b):
    return lax.dot_general(a, b, (((1,), (1,)), ((), ())), preferred_element_type=F32)


def _split_bf16(x):
    hi = x.astype(BF16)
    lo = (x - hi.astype(F32)).astype(BF16)
    return hi, lo


def _ada_kernel(c_ref, w_ref, b_ref, o_ref):
    c = c_ref[...]
    h = (c * jax.nn.sigmoid(c)).astype(BF16)
    o_ref[...] = _dot(h, w_ref[...].astype(BF16)) + b_ref[...]


def _ada(c, w, b):
    rows, d = c.shape
    n = w.shape[1]
    return pl.pallas_call(
        _ada_kernel,
        grid=(n // d,),
        in_specs=[pl.BlockSpec((rows, d), lambda j: (0, 0)),
                  pl.BlockSpec((d, d), lambda j: (0, j)),
                  pl.BlockSpec((1, d), lambda j: (0, j))],
        out_specs=pl.BlockSpec((rows, d), lambda j: (0, j)),
        out_shape=jax.ShapeDtypeStruct((rows, n), F32),
        compiler_params=_cparams(("arbitrary",)),
        name="ada",
    )(c, w, b.reshape(1, n))


def _norm_mod(x, nw, sh, sc):
    ms = jnp.mean(x * x, axis=-1, keepdims=True)
    y = x * lax.rsqrt(ms + RMS_EPS) * nw
    return y * (1.0 + sc) + sh


def _inproj_prompt_kernel(x_ref, sh_ref, sc_ref, nw_ref, w_ref, wt_ref,
                          sbq_o, sbk_o, sbv_o, dq_o, dk_o, dv_o,
                          sbk_t, sbv_t, dk_f, dv_f):
    u = _norm_mod(x_ref[...], nw_ref[...], sh_ref[0], sc_ref[0]).astype(BF16)
    tm = u.shape[0]
    hm_outs = (sbq_o, sbk_o, sbv_o, dq_o, dk_o, None)
    t_outs = (None, sbk_t, sbv_t, None, None, None)
    f_outs = (None, None, None, None, dk_f, dv_f)
    scales = (QK_SCALE, None, None, QK_SCALE, None, None)
    for piece in range(N_PIECES):
        p = _dot(u, w_ref[:, piece * PIECE:(piece + 1) * PIECE])
        if t_outs[piece] is not None:
            t_outs[piece][0] = _dot_nt(wt_ref[(piece - 1) * PIECE:piece * PIECE, :], u)
        if f_outs[piece] is not None:
            for h in range(DIFF_HEADS):
                f_outs[piece][pl.ds(h, tm, stride=DIFF_HEADS), :] = p[:, h * DIFF_V_DIM:(h + 1) * DIFF_V_DIM]
        if scales[piece] is not None:
            p = p * scales[piece]
        pb = p.astype(BF16)
        if hm_outs[piece] is not None:
            for h in range(PIECE // HEAD_DIM):
                hm_outs[piece][0, h] = pb[:, h * HEAD_DIM:(h + 1) * HEAD_DIM]
        else:
            for h in range(DIFF_HEADS):
                dv_o[0, h] = pb[:, h * DIFF_V_DIM:(h + 1) * DIFF_V_DIM]


def _inproj_prompt(x, sh, sc, nw, w_bf, wt_sbkv, batch, seq):
    n, d = x.shape
    tm = min(TOKEN_BLOCK, seq)
    nb = seq // tm
    hm = lambda i: (i // nb, 0, i % nb, 0)
    row = lambda i: (i, 0)
    mod = lambda i: (i // nb, 0, 0)
    hm64 = jax.ShapeDtypeStruct((batch, 8, seq, HEAD_DIM), BF16)
    hm128 = jax.ShapeDtypeStruct((batch, DIFF_HEADS, seq, DIFF_V_DIM), BF16)
    transposed = jax.ShapeDtypeStruct((batch, PIECE, seq), F32)
    head_rows = jax.ShapeDtypeStruct((n * DIFF_HEADS, DIFF_V_DIM), F32)
    return pl.pallas_call(
        _inproj_prompt_kernel,
        grid=(n // tm,),
        in_specs=[pl.BlockSpec((tm, d), row),
                  pl.BlockSpec((1, 1, d), mod),
                  pl.BlockSpec((1, 1, d), mod),
                  pl.BlockSpec((1, d), lambda i: (0, 0)),
                  pl.BlockSpec((d, N_PIECES * PIECE), lambda i: (0, 0)),
                  pl.BlockSpec((2 * PIECE, d), lambda i: (0, 0))],
        out_specs=[pl.BlockSpec((1, 8, tm, HEAD_DIM), hm)] * 5
                  + [pl.BlockSpec((1, DIFF_HEADS, tm, DIFF_V_DIM), hm)]
                  + [pl.BlockSpec((1, PIECE, tm), lambda i: (i // nb, 0, i % nb))] * 2
                  + [pl.BlockSpec((tm * DIFF_HEADS, DIFF_V_DIM), row)] * 2,
        out_shape=[hm64] * 5 + [hm128] + [transposed] * 2 + [head_rows] * 2,
        compiler_params=_cparams(("arbitrary",), VMEM_LIMIT),
        name="inproj_prompt",
    )(x, sh, sc, nw, w_bf, wt_sbkv)


def _inproj_sample_kernel(x_ref, sh_ref, sc_ref, nw_ref, w_ref, o_ref):
    u = _norm_mod(x_ref[...], nw_ref[...], sh_ref[...], sc_ref[...]).astype(BF16)
    for piece in range(N_PIECES):
        sl = slice(piece * PIECE, (piece + 1) * PIECE)
        o_ref[:, sl] = _dot(u, w_ref[:, sl])


def _inproj_sample(x, sh, sc, nw, w_bf):
    n, d = x.shape
    tm = min(TOKEN_BLOCK, n)
    row = lambda i: (i, 0)
    return pl.pallas_call(
        _inproj_sample_kernel,
        grid=(n // tm,),
        in_specs=[pl.BlockSpec((tm, d), row), pl.BlockSpec((tm, d), row), pl.BlockSpec((tm, d), row),
                  pl.BlockSpec((1, d), lambda i: (0, 0)),
                  pl.BlockSpec((d, N_PIECES * PIECE), lambda i: (0, 0))],
        out_specs=pl.BlockSpec((tm, N_PIECES * PIECE), row),
        out_shape=jax.ShapeDtypeStruct((n, N_PIECES * PIECE), F32),
        compiler_params=_cparams(("arbitrary",), VMEM_LIMIT),
        name="inproj_sample",
    )(x, sh, sc, nw, w_bf)


def _sb_logs(z):
    t = jnp.log1p(jnp.exp(-jnp.abs(z)))
    return jnp.minimum(z, 0.0) - t, -jnp.maximum(z, 0.0) - t


def _strict_upper(t):
    r = lax.broadcasted_iota(I32, (t, t), 0)
    c = lax.broadcasted_iota(I32, (t, t), 1)
    return jnp.where(r > c, 1.0, 0.0).astype(BF16)


def _suffix_sum(ls, tri):
    hi, lo = _split_bf16(ls)
    return _dot(hi, tri) + _dot(lo, tri)


def _sb_prompt_kernel(q_ref, k_ref, v_ref, o_ref, carry_ref, acc_ref, *, blk):
    i = pl.program_id(1)
    tri = _strict_upper(blk)
    r = lax.broadcasted_iota(I32, (blk, blk), 0)
    c = lax.broadcasted_iota(I32, (blk, blk), 1)
    vis = c < r

    def block(j, diag):
        start = pl.multiple_of(j * blk, blk)
        for h in range(SB_HEADS):
            k = k_ref[0, h, pl.ds(start, blk), :]
            v = v_ref[0, h, pl.ds(start, blk), :]
            lb, ls = _sb_logs(_dot_nt(q_ref[0, h], k))
            if diag:
                ls = jnp.where(vis, ls, 0.0)
            carry = carry_ref[h]
            a = jnp.exp(lb + _suffix_sum(ls, tri) + carry)
            if diag:
                a = jnp.where(vis, a, 0.0)
            acc_ref[h] = acc_ref[h] + _dot(a.astype(BF16), v)
            carry_ref[h] = carry + jnp.sum(ls, axis=1, keepdims=True)

    carry_ref[...] = jnp.zeros_like(carry_ref)
    acc_ref[...] = jnp.zeros_like(acc_ref)
    block(i, True)

    def cond(s):
        return jnp.logical_and(s[0] >= 0, s[1] > SB_SKIP_BELOW)

    def body(s):
        block(s[0], False)
        return s[0] - 1, jnp.max(carry_ref[...])

    lax.while_loop(cond, body, (i - 1, jnp.max(carry_ref[...])))
    o_ref[0] = jnp.concatenate([acc_ref[h] for h in range(SB_HEADS)], axis=1).astype(o_ref.dtype)


def _sb_prompt(q, k, v):
    b, h, s, dh = q.shape
    blk = min(ATTN_BLOCK, s)
    return pl.pallas_call(
        functools.partial(_sb_prompt_kernel, blk=blk),
        grid=(b, s // blk),
        in_specs=[pl.BlockSpec((1, h, blk, dh), lambda bi, i: (bi, 0, i, 0)),
                  pl.BlockSpec((1, h, s, dh), lambda bi, i: (bi, 0, 0, 0)),
                  pl.BlockSpec((1, h, s, dh), lambda bi, i: (bi, 0, 0, 0))],
        out_specs=pl.BlockSpec((1, blk, h * dh), lambda bi, i: (bi, i, 0)),
        out_shape=jax.ShapeDtypeStruct((b, s, h * dh), BF16),
        scratch_shapes=[pltpu.VMEM((h, blk, 1), F32), pltpu.VMEM((h, blk, dh), F32)],
        compiler_params=_cparams(("arbitrary", "arbitrary"), VMEM_LIMIT),
        name="sb_prompt",
    )(q, k, v)


def _lambda_from(lamv_ref, lambda_init):
    l1 = jnp.sum(lamv_ref[0:1, :] * lamv_ref[1:2, :], axis=1, keepdims=True)
    l2 = jnp.sum(lamv_ref[2:3, :] * lamv_ref[3:4, :], axis=1, keepdims=True)
    return jnp.exp(l1) - jnp.exp(l2) + lambda_init


def _head_slope(h):
    out = jnp.float32(ALIBI_SLOPES[-1])
    for hh in range(DIFF_HEADS - 2, -1, -1):
        out = jnp.where(h == hh, jnp.float32(ALIBI_SLOPES[hh]), out)
    return out


def _diff_prompt_kernel(lamv_ref, sw_ref, q_ref, k_ref, v_ref, o_ref, *, blk, lambda_init):
    h = pl.program_id(1)
    i = pl.program_id(2)
    slope = _head_slope(h)
    r = lax.broadcasted_iota(I32, (blk, blk), 0)
    c = lax.broadcasted_iota(I32, (blk, blk), 1)
    srel = slope * (r - c).astype(F32)
    causal = c <= r
    qs = (q_ref[0, 0], q_ref[0, 1])

    def block(j, st, diag):
        start = pl.multiple_of(j * blk, blk)
        v = v_ref[0, 0, pl.ds(start, blk), :]
        off = slope * ((i - j) * blk).astype(F32)
        new = []
        for comp in range(2):
            m, l, acc = st[3 * comp:3 * comp + 3]
            k = k_ref[0, comp, pl.ds(start, blk), :]
            s = _dot_nt(qs[comp], k) - srel
            if diag:
                s = jnp.where(causal, s, NEG_BIG)
            m_new = jnp.maximum(m, jnp.max(s, axis=1, keepdims=True) - off)
            p = jnp.exp(s - (m_new + off))
            alpha = jnp.exp(m - m_new)
            l = l * alpha + jnp.sum(p, axis=1, keepdims=True)
            acc = acc * alpha + _dot(p.astype(BF16), v)
            new += [m_new, l, acc]
        return tuple(new)

    init = (jnp.full((blk, 1), NEG_BIG, F32), jnp.zeros((blk, 1), F32),
            jnp.zeros((blk, DIFF_V_DIM), F32)) * 2
    st = lax.fori_loop(0, i, lambda j, st: block(j, st, False), init)
    st = block(i, st, True)
    lam = _lambda_from(lamv_ref, lambda_init)
    o = st[2] / st[1] - lam * (st[5] / st[4])
    o = o * lax.rsqrt(jnp.mean(o * o, axis=1, keepdims=True) + RMS_EPS)
    o_ref[0] = (o * sw_ref[...] * (1.0 - lambda_init)).astype(o_ref.dtype)


def _diff_prompt(lamv, sw, q, k, v, lambda_init):
    b, h2, s, dh = q.shape
    blk = min(DIFF_BLOCK, s)
    return pl.pallas_call(
        functools.partial(_diff_prompt_kernel, blk=blk, lambda_init=lambda_init),
        grid=(b, DIFF_HEADS, s // blk),
        in_specs=[pl.BlockSpec((4, HEAD_DIM), lambda bi, hi, i: (0, 0)),
                  pl.BlockSpec((1, DIFF_V_DIM), lambda bi, hi, i: (0, 0)),
                  pl.BlockSpec((1, 2, blk, dh), lambda bi, hi, i: (bi, hi, i, 0)),
                  pl.BlockSpec((1, 2, s, dh), lambda bi, hi, i: (bi, hi, 0, 0)),
                  pl.BlockSpec((1, 1, s, DIFF_V_DIM), lambda bi, hi, i: (bi, hi, 0, 0))],
        out_specs=pl.BlockSpec((1, blk, DIFF_V_DIM), lambda bi, hi, i: (bi, i, hi)),
        out_shape=jax.ShapeDtypeStruct((b, s, DIFF_V_WIDTH), BF16),
        compiler_params=_cparams(("arbitrary", "arbitrary", "arbitrary"), VMEM_LIMIT),
        name="diff_prompt",
    )(lamv, sw, q, k, v)


def _sample_kernel(pt_ref, lamv_ref, sw_ref, qsb_ref, qd_ref, new_ref, *rest,
                   n_new, page, n_pages, pages_per_step, lambda_init):
    g_ = pages_per_step
    sbk_pages = rest[0 * g_:1 * g_]
    sbv_pages = rest[1 * g_:2 * g_]
    dk_pages = rest[2 * g_:3 * g_]
    dv_pages = rest[3 * g_:4 * g_]
    o_ref = rest[4 * g_]
    sb_carry, sb_acc, d_m, d_l, d_acc = rest[4 * g_ + 1:]
    t = pl.program_id(1)
    rows = 8 * n_new
    hrows = 2 * n_new
    past_len = n_pages * page

    ridx = lax.broadcasted_iota(I32, (rows, 1), 0)
    qidx = ridx % n_new
    grp = ridx // n_new
    slope_row = jnp.zeros((rows, 1), F32)
    for hh in range(DIFF_HEADS):
        slope_row = jnp.where(grp // 2 == hh, jnp.float32(ALIBI_SLOPES[hh]), slope_row)
    qpos = (past_len + qidx).astype(F32)

    @pl.when(t == 0)
    def _():
        qsb = qsb_ref[0].astype(F32)
        col = lax.broadcasted_iota(I32, (rows, DIFF_QK_WIDTH), 1)
        qd = jnp.where(col // DIFF_V_DIM == grp // 2,
                       jnp.concatenate([qd_ref[0].astype(F32)] * DIFF_HEADS, axis=1), 0.0)
        carry = jnp.zeros((rows, 1), F32)
        acc = jnp.zeros((rows, SB_WIDTH), F32)
        m = jnp.full((rows, 1), NEG_BIG, F32)
        l = jnp.zeros((rows, 1), F32)
        dacc = jnp.zeros((rows, DIFF_V_DIM), F32)
        for j in range(n_new - 1, -1, -1):
            sbk = new_ref[0, j:j + 1, 1 * PIECE:2 * PIECE]
            sbv = new_ref[0, j:j + 1, 2 * PIECE:3 * PIECE]
            dk = new_ref[0, j:j + 1, 4 * PIECE:5 * PIECE]
            dv = new_ref[0, j:j + 1, 5 * PIECE:6 * PIECE]
            z = jnp.sum(qsb * sbk, axis=1, keepdims=True)
            lb, ls = _sb_logs(z)
            vis = j < qidx
            ls = jnp.where(vis, ls, 0.0)
            a = jnp.where(vis, jnp.exp(lb + carry), 0.0)
            acc = acc + a * sbv
            carry = carry + ls
            s = jnp.sum(qd * dk, axis=1, keepdims=True) - slope_row * (qidx - j).astype(F32)
            s = jnp.where(j <= qidx, s, NEG_BIG)
            m_new = jnp.maximum(m, s)
            p = jnp.exp(s - m_new)
            alpha = jnp.exp(m - m_new)
            l = l * alpha + p
            dv_rows = jnp.concatenate(
                [jnp.broadcast_to(dv[:, hh * DIFF_V_DIM:(hh + 1) * DIFF_V_DIM], (hrows, DIFF_V_DIM))
                 for hh in range(DIFF_HEADS)], axis=0)
            dacc = dacc * alpha + p * dv_rows
            m = m_new
        sb_carry[...] = carry
        sb_acc[...] = acc
        d_m[...] = m
        d_l[...] = l
        d_acc[...] = dacc

    @pl.when(jnp.max(sb_carry[...]) > SB_SKIP_BELOW)
    def _():
        qsb = qsb_ref[0]
        tri = _strict_upper(page)
        z = jnp.concatenate([_dot(qsb, sbk_pages[g][0].astype(BF16)) for g in range(g_)], axis=0)
        lb, ls = _sb_logs(z)
        suffix = _suffix_sum(ls, tri)
        rowsum = jnp.sum(ls, axis=1, keepdims=True)
        carry = sb_carry[...]
        carries = []
        for g in range(g_):
            carries.append(carry)
            carry = carry + rowsum[g * rows:(g + 1) * rows]
        a = jnp.exp(lb + suffix + jnp.concatenate(carries, axis=0)).astype(BF16)
        acc = sb_acc[...]
        for g in range(g_):
            acc = acc + _dot_nt(a[g * rows:(g + 1) * rows], sbv_pages[g][0].astype(BF16))
        sb_acc[...] = acc
        sb_carry[...] = carry

    qd = qd_ref[0]
    lane = lax.broadcasted_iota(I32, (1, page), 1).astype(F32)
    ss, bases = [], []
    for g in range(g_):
        pg = n_pages - 1 - (t * g_ + g)
        for hh in range(DIFF_HEADS):
            kh = dk_pages[g][0, pl.ds(hh, page, stride=DIFF_HEADS), :].astype(BF16)
            ss.append(_dot_nt(qd[hh * hrows:(hh + 1) * hrows], kh))
        bases.append(qpos - (pg * page).astype(F32))
    s = jnp.concatenate(ss, axis=0)
    base = jnp.concatenate(bases, axis=0)
    slope_all = jnp.concatenate([slope_row] * g_, axis=0)
    s = s - slope_all * (base - lane)
    rowmax = jnp.max(s, axis=1, keepdims=True)
    m_old = d_m[...]
    m_new = m_old
    for g in range(g_):
        m_new = jnp.maximum(m_new, rowmax[g * rows:(g + 1) * rows])
    p = jnp.exp(s - jnp.concatenate([m_new] * g_, axis=0))
    rowsum = jnp.sum(p, axis=1, keepdims=True)
    alpha = jnp.exp(m_old - m_new)
    l = d_l[...] * alpha
    for g in range(g_):
        l = l + rowsum[g * rows:(g + 1) * rows]
    pb = p.astype(BF16)
    pvs = []
    for hh in range(DIFF_HEADS):
        pv = jnp.zeros((hrows, DIFF_V_DIM), F32)
        for g in range(g_):
            vh = dv_pages[g][0, pl.ds(hh, page, stride=DIFF_HEADS), :].astype(BF16)
            pv = pv + _dot(pb[g * rows + hh * hrows:g * rows + (hh + 1) * hrows], vh)
        pvs.append(pv)
    d_acc[...] = d_acc[...] * alpha + jnp.concatenate(pvs, axis=0)
    d_l[...] = l
    d_m[...] = m_new

    @pl.when(t == pl.num_programs(1) - 1)
    def _():
        col = lax.broadcasted_iota(I32, (rows, SB_WIDTH), 1)
        qrow = lax.broadcasted_iota(I32, (8, rows), 0)
        rcol = lax.broadcasted_iota(I32, (8, rows), 1)
        sel = jnp.where(rcol % n_new == qrow, 1.0, 0.0).astype(BF16)
        hi, lo = _split_bf16(jnp.where(col // HEAD_DIM == grp, sb_acc[...], 0.0))
        o_sb = _dot(sel, hi) + _dot(sel, lo)

        lam = _lambda_from(lamv_ref, lambda_init)
        on = d_acc[...] / d_l[...]
        segs = []
        for hh in range(DIFF_HEADS):
            both = on[hh * hrows:(hh + 1) * hrows]
            seg = both - lam * pltpu.roll(both, n_new, axis=0)
            seg = seg * lax.rsqrt(jnp.mean(seg * seg, axis=1, keepdims=True) + RMS_EPS)
            segs.append(seg * sw_ref[...] * (1.0 - lambda_init))
        full = jnp.concatenate([o_sb] + segs, axis=1)
        o_ref[0] = full[:n_new].astype(o_ref.dtype)


def _sample_attention(page_table, lamv, sw, qsb_bd, qd_bd, p_new, caches, lambda_init):
    nseq, n_pages = page_table.shape
    n_new = p_new.shape[1]
    assert 2 * n_new == 8, "differential rows per head must fill one sublane group"
    rows_per_page = caches[0].shape[1]
    page = caches[0].shape[2]
    g_ = math.gcd(PAGES_PER_STEP, n_pages)
    rows = 8 * n_new

    def page_spec(g):
        return pl.BlockSpec((1, rows_per_page, page),
                            lambda b, t, pt, g=g: (pt[b, n_pages - 1 - (t * g_ + g)], 0, 0))

    cache_specs, cache_args = [], []
    for cache in caches:
        for g in range(g_):
            cache_specs.append(page_spec(g))
            cache_args.append(cache)
    grid_spec = pltpu.PrefetchScalarGridSpec(
        num_scalar_prefetch=1,
        grid=(nseq, n_pages // g_),
        in_specs=[pl.BlockSpec((4, HEAD_DIM), lambda b, t, pt: (0, 0)),
                  pl.BlockSpec((1, DIFF_V_DIM), lambda b, t, pt: (0, 0)),
                  pl.BlockSpec((1, rows, SB_WIDTH), lambda b, t, pt: (b, 0, 0)),
                  pl.BlockSpec((1, rows, DIFF_V_DIM), lambda b, t, pt: (b, 0, 0)),
                  pl.BlockSpec((1, n_new, N_PIECES * PIECE), lambda b, t, pt: (b, 0, 0))] + cache_specs,
        out_specs=pl.BlockSpec((1, n_new, 2 * PIECE), lambda b, t, pt: (b, 0, 0)),
        scratch_shapes=[pltpu.VMEM((rows, 1), F32), pltpu.VMEM((rows, SB_WIDTH), F32),
                        pltpu.VMEM((rows, 1), F32), pltpu.VMEM((rows, 1), F32),
                        pltpu.VMEM((rows, DIFF_V_DIM), F32)],
    )
    return pl.pallas_call(
        functools.partial(_sample_kernel, n_new=n_new, page=page, n_pages=n_pages,
                          pages_per_step=g_, lambda_init=lambda_init),
        grid_spec=grid_spec,
        out_shape=jax.ShapeDtypeStruct((nseq, n_new, 2 * PIECE), BF16),
        compiler_params=_cparams(("arbitrary", "arbitrary"), VMEM_LIMIT),
        name="sample_attention",
    )(page_table, lamv, sw, qsb_bd, qd_bd, p_new, *cache_args)


def _sb_query_rows(q):
    nseq, n_new, w = q.shape
    tiled = jnp.broadcast_to(q[:, None], (nseq, 8, n_new, w)).reshape(nseq, 8 * n_new, w)
    grp = (jnp.arange(8 * n_new) // n_new)[:, None]
    col = (jnp.arange(w) // HEAD_DIM)[None, :]
    return jnp.where(grp == col, tiled * QK_SCALE, 0.0).astype(BF16)


def _diff_query_rows(q):
    nseq, n_new, _ = q.shape
    q5 = q.reshape(nseq, n_new, DIFF_HEADS, 2, HEAD_DIM) * QK_SCALE
    q5 = jnp.transpose(q5, (0, 2, 3, 1, 4))
    eye = jnp.eye(2, dtype=q.dtype)[None, None, :, None, :, None]
    out = q5[:, :, :, :, None, :] * eye
    return out.reshape(nseq, 8 * n_new, DIFF_V_DIM).astype(BF16)


def _post_kernel(x_ref, osb_ref, od_ref, g_ref, sh_ref, sc_ref, nw_ref, wo_ref, rw_hi_ref, rw_lo_ref,
                 rb_ref, h_ref, f_ref, gates_ref, *, per_token):
    take = (lambda r: r[...]) if per_token else (lambda r: r[0])
    proj = _dot(osb_ref[...], wo_ref[:SB_WIDTH, :]) + _dot(od_ref[...], wo_ref[SB_WIDTH:, :])
    h = x_ref[...] + take(g_ref) * proj
    h_ref[...] = h
    f = _norm_mod(h, nw_ref[...], take(sh_ref), take(sc_ref))

    f_ref[...] = f

    f_hi, f_lo = _split_bf16(f)
    logits = (_dot(f_hi, rw_hi_ref[...]) + _dot(f_lo, rw_hi_ref[...]) + _dot(f_hi, rw_lo_ref[...])
              + rb_ref[...])
    lane = lax.broadcasted_iota(I32, logits.shape, 1).astype(F32)
    work = logits
    picks, vals = [], []
    for _ in range(TOP_K):
        mx = jnp.max(work, axis=1, keepdims=True)
        first = jnp.min(jnp.where(work == mx, lane, float(N_EXPERTS)), axis=1, keepdims=True)
        pick = lane == first
        picks.append(pick)
        vals.append(mx)
        work = jnp.where(pick, -jnp.inf, work)
    es = [jnp.exp(v - vals[0]) for v in vals]
    denom = es[0] + es[1] + es[2] + es[3]
    gates = jnp.zeros_like(logits)
    for pick, e in zip(picks, es):
        gates = jnp.where(pick, e / denom, gates)
    gates_ref[...] = gates


def _post(x, osb, od, g, sh, sc, nw, wo_bf, rw_hi, rw_lo, rb, tokens_per_group):
    n, d = x.shape
    per_token = tokens_per_group is None
    tm = min(TOKEN_BLOCK, n if per_token else tokens_per_group)
    row = lambda i: (i, 0)
    const = lambda i: (0, 0)
    if per_token:
        mod_spec = pl.BlockSpec((tm, d), row)
    else:
        nb = tokens_per_group // tm
        mod_spec = pl.BlockSpec((1, 1, d), lambda i: (i // nb, 0, 0))
    return pl.pallas_call(
        functools.partial(_post_kernel, per_token=per_token),
        grid=(n // tm,),
        in_specs=[pl.BlockSpec((tm, d), row),
                  pl.BlockSpec((tm, SB_WIDTH), row),
                  pl.BlockSpec((tm, DIFF_V_WIDTH), row),
                  mod_spec, mod_spec, mod_spec,
                  pl.BlockSpec((1, d), const),
                  pl.BlockSpec((SB_WIDTH + DIFF_V_WIDTH, d), const),
                  pl.BlockSpec((d, N_EXPERTS), const),
                  pl.BlockSpec((d, N_EXPERTS), const),
                  pl.BlockSpec((1, N_EXPERTS), const)],
        out_specs=[pl.BlockSpec((tm, d), row),
                   pl.BlockSpec((tm, d), row),
                   pl.BlockSpec((tm, N_EXPERTS), row)],
        out_shape=[jax.ShapeDtypeStruct((n, d), F32),
                   jax.ShapeDtypeStruct((n, d), F32),
                   jax.ShapeDtypeStruct((n, N_EXPERTS), F32)],
        compiler_params=_cparams(("arbitrary",), VMEM_LIMIT),
        name="post_attention",
    )(x, osb, od, g, sh, sc, nw, wo_bf, rw_hi, rw_lo, rb)


def _moe_plan(n_tokens):
    nblk = 1
    while n_tokens % nblk or n_tokens // nblk > MOE_MAX_TOKENS or (n_tokens // nblk) % 8:
        nblk += 1
    ttok = n_tokens // nblk
    mc = min(MOE_CHUNK, ttok)
    smax = N_EXPERTS + (ttok * TOP_K) // mc
    return nblk, ttok, mc, smax


def _routing(gates, nblk, ttok, mc, smax):
    sel = (gates > 0).reshape(nblk, ttok, N_EXPERTS)
    cnt = jnp.sum(sel.astype(I32), axis=1)
    nch = (cnt + mc - 1) // mc
    ends = jnp.cumsum(nch, axis=1)
    starts = ends - nch
    total = ends[:, -1]
    s_ar = jnp.arange(smax, dtype=I32)
    slot_e = jnp.sum((s_ar[None, :, None] >= ends[:, None, :]).astype(I32), axis=-1)
    valid = s_ar[None, :] < total[:, None]
    last_e = jnp.take_along_axis(slot_e, jnp.maximum(total - 1, 0)[:, None], axis=1)
    slot_e = jnp.where(valid, jnp.minimum(slot_e, N_EXPERTS - 1), jnp.minimum(last_e, N_EXPERTS - 1))
    chunk = s_ar[None, :] - jnp.take_along_axis(starts, slot_e, axis=1)
    slot_cnt = jnp.where(valid, jnp.clip(jnp.take_along_axis(cnt, slot_e, axis=1) - chunk * mc, 0, mc), 0)
    order = jnp.argsort(jnp.logical_not(jnp.swapaxes(sel, 1, 2)), axis=2, stable=True).astype(I32)
    lane = jnp.arange(mc, dtype=I32)
    pos = jnp.minimum(chunk[:, :, None] * mc + lane[None, None, :], ttok - 1)
    flat = (jnp.arange(nblk, dtype=I32)[:, None, None] * N_EXPERTS + slot_e[:, :, None]) * ttok + pos
    tok = jnp.take(order.reshape(-1), flat.reshape(-1)).reshape(nblk, smax, mc)
    live = lane[None, None, :] < slot_cnt[:, :, None]
    gidx = (jnp.arange(nblk, dtype=I32)[:, None, None] * ttok + jnp.where(live, tok, 0)) * N_EXPERTS + slot_e[:, :, None]
    gw = jnp.where(live, jnp.take(gates.reshape(-1), gidx.reshape(-1)).reshape(nblk, smax, mc), 0.0)
    tok = jnp.where(live, tok, ttok + lane[None, None, :] % MOE_PAD_ROWS)
    return slot_e.reshape(-1), slot_cnt.reshape(-1).astype(I32), tok, gw


def _moe_kernel(se_ref, sc_ref, x_hbm, idx_ref, gw_ref, wgu_ref, bgu_ref, wd_ref, bd_ref, out_hbm,
                x_ref, acc_ref, xg_ref, y_ref, sem, *, ttok, mc, smax, d_ff):
    tb = pl.program_id(0)
    s = pl.program_id(1)
    cnt = sc_ref[tb * smax + s]

    @pl.when(jnp.logical_and(tb == 0, s == 0))
    def _():
        xg_ref[...] = jnp.zeros_like(xg_ref)

    @pl.when(s == 0)
    def _():
        cp = pltpu.make_async_copy(x_hbm.at[pl.ds(tb * ttok, ttok), :], x_ref.at[pl.ds(0, ttok), :], sem.at[0])
        cp.start()
        x_ref[pl.ds(ttok, MOE_PAD_ROWS), :] = jnp.zeros((MOE_PAD_ROWS, x_ref.shape[1]), F32)
        acc_ref[...] = jnp.zeros_like(acc_ref)
        cp.wait()

    @pl.when(cnt > 0)
    def _():
        ngrp = (cnt + MOE_PAD_ROWS - 1) // MOE_PAD_ROWS

        def gather(gi, carry):
            for u in range(MOE_PAD_ROWS):
                r = gi * MOE_PAD_ROWS + u
                xg_ref[pl.ds(r, 1), :] = x_ref[pl.ds(idx_ref[0, 0, r], 1), :]
            return carry

        lax.fori_loop(0, ngrp, gather, 0)
        gu = _dot(xg_ref[...].astype(BF16), wgu_ref[0]) + bgu_ref[0]
        g = jnp.minimum(gu[:, :d_ff], SWIGLU_LIMIT)
        u = jnp.clip(gu[:, d_ff:], -SWIGLU_LIMIT, SWIGLU_LIMIT)
        act = ((u + 1.0) * g * jax.nn.sigmoid(SWIGLU_ALPHA * g)).astype(BF16)
        y_ref[...] = _dot(act, wd_ref[0]) + bd_ref[0]

        def scatter(gi, carry):
            for half in range(2):
                rs = [gi * MOE_PAD_ROWS + half * 4 + u_ for u_ in range(4)]
                toks = [idx_ref[0, 0, r] for r in rs]
                new = [acc_ref[pl.ds(t_, 1), :] + gw_ref[0, 0, r] * y_ref[pl.ds(r, 1), :]
                       for t_, r in zip(toks, rs)]
                for t_, v_ in zip(toks, new):
                    acc_ref[pl.ds(t_, 1), :] = v_
            return carry

        lax.fori_loop(0, ngrp, scatter, 0)

    @pl.when(s == smax - 1)
    def _():
        cp = pltpu.make_async_copy(acc_ref.at[pl.ds(0, ttok), :], out_hbm.at[pl.ds(tb * ttok, ttok), :],
                                   sem.at[1])
        cp.start()
        cp.wait()


def _moe(x, gates, wgu_bf, bgu, wd_bf, bd):
    n, d = x.shape
    d_ff = wd_bf.shape[1]
    nblk, ttok, mc, smax = _moe_plan(n)
    slot_e, slot_cnt, tok, gw = _routing(gates, nblk, ttok, mc, smax)
    slot = lambda tb, s, se, sc: (tb * smax + s, 0, 0)
    expert = lambda tb, s, se, sc: (se[tb * smax + s], 0, 0)
    grid_spec = pltpu.PrefetchScalarGridSpec(
        num_scalar_prefetch=2,
        grid=(nblk, smax),
        in_specs=[pl.BlockSpec(memory_space=pl.ANY),
                  pl.BlockSpec((1, 1, mc), slot, memory_space=pltpu.SMEM),
                  pl.BlockSpec((1, 1, mc), slot, memory_space=pltpu.SMEM),
                  pl.BlockSpec((1, d, 2 * d_ff), expert),
                  pl.BlockSpec((1, 1, 2 * d_ff), expert),
                  pl.BlockSpec((1, d_ff, d), expert),
                  pl.BlockSpec((1, 1, d), expert)],
        out_specs=pl.BlockSpec(memory_space=pl.ANY),
        scratch_shapes=[pltpu.VMEM((ttok + MOE_PAD_ROWS, d), F32), pltpu.VMEM((ttok + MOE_PAD_ROWS, d), F32),
                        pltpu.VMEM((mc, d), F32),
                        pltpu.VMEM((mc, d), F32), pltpu.SemaphoreType.DMA((2,))],
    )
    return pl.pallas_call(
        functools.partial(_moe_kernel, ttok=ttok, mc=mc, smax=smax, d_ff=d_ff),
        grid_spec=grid_spec,
        out_shape=jax.ShapeDtypeStruct((n, d), F32),
        compiler_params=_cparams(("arbitrary", "arbitrary"), VMEM_LIMIT),
        name="moe",
    )(slot_e, slot_cnt, x, tok.reshape(nblk * smax, 1, mc), gw.reshape(nblk * smax, 1, mc),
      wgu_bf, bgu, wd_bf, bd)


def _final_kernel(h_ref, f_ref, g_ref, nw_ref, o_ref, *, per_token, last):
    g = g_ref[...] if per_token else g_ref[0]
    h = h_ref[...] + g * f_ref[...]
    if last:
        h = h * lax.rsqrt(jnp.mean(h * h, axis=-1, keepdims=True) + RMS_EPS) * nw_ref[...]
    o_ref[...] = h


def _final(h, f_all, row0, g, nw, tokens_per_group, last):
    n, d = h.shape
    per_token = tokens_per_group is None
    tm = min(TOKEN_BLOCK, n if per_token else tokens_per_group)
    assert row0 % tm == 0
    b0 = row0 // tm
    row = lambda i: (i, 0)
    if per_token:
        g_spec = pl.BlockSpec((tm, d), row)
    else:
        nb = tokens_per_group // tm
        g_spec = pl.BlockSpec((1, 1, d), lambda i: (i // nb, 0, 0))
    return pl.pallas_call(
        functools.partial(_final_kernel, per_token=per_token, last=last),
        grid=(n // tm,),
        in_specs=[pl.BlockSpec((tm, d), row),
                  pl.BlockSpec((tm, d), lambda i: (b0 + i, 0)),
                  g_spec,
                  pl.BlockSpec((1, d), lambda i: (0, 0))],
        out_specs=pl.BlockSpec((tm, d), row),
        out_shape=jax.ShapeDtypeStruct((n, d), F32),
        compiler_params=_cparams(("arbitrary",), VMEM_LIMIT),
        name="final",
    )(h, f_all, g, nw)


def kernel(x_prompt, x_sample, cache_sb_k, cache_sb_v, cache_diff_k, cache_diff_v, page_table, c_prompt, c_sample, ada_w, ada_b, norm_attn_w, w_in, diff_lambda_q1, diff_lambda_k1, diff_lambda_q2, diff_lambda_k2, diff_subln_w, w_out, norm_ffn_w, router_w, router_b, w_gate_up, b_gate_up, w_down, b_down, final_norm_w):
    batch, seq, d = x_prompt.shape
    nseq, n_new, _ = x_sample.shape
    depth = ada_w.shape[0]
    n_p, n_s = batch * seq, nseq * n_new
    n_pool, page = cache_sb_k.shape[1], cache_sb_k.shape[2]

    hp = x_prompt.reshape(n_p, d)
    hs = x_sample.reshape(n_s, d)
    c_all = jnp.concatenate([c_prompt, c_sample], axis=0)
    new_kv = [[] for _ in range(8)]
    final_w = final_norm_w.reshape(1, d)

    for l in range(depth):
        lambda_init = 0.8 - 0.6 * math.exp(-0.3 * l)
        ada = _ada(c_all, ada_w[l], ada_b[l])
        mods_p = [ada[:batch, i * d:(i + 1) * d].reshape(batch, 1, d) for i in range(6)]
        mods_s = [jnp.repeat(ada[batch:, i * d:(i + 1) * d], n_new, axis=0) for i in range(6)]
        lamv = jnp.stack([diff_lambda_q1[l], diff_lambda_k1[l], diff_lambda_q2[l], diff_lambda_k2[l]]).astype(F32)
        sw = diff_subln_w[l].reshape(1, DIFF_V_DIM).astype(F32)
        nw_a = norm_attn_w[l].reshape(1, d)
        nw_f = norm_ffn_w[l].reshape(1, d)
        w_in_bf = w_in[l].astype(BF16)
        w_out_bf = w_out[l].astype(BF16)
        rw_hi = router_w[l].astype(BF16)
        rw_lo = (router_w[l] - rw_hi.astype(F32)).astype(BF16)
        rb = router_b[l].reshape(1, N_EXPERTS)

        (sbq, sbk, sbv, dq, dk, dv, sbk_t, sbv_t, dk_f, dv_f) = _inproj_prompt(
            hp, mods_p[0], mods_p[1], nw_a, w_in_bf, w_in_bf[:, PIECE:3 * PIECE].T, batch, seq)
        o_sb = _sb_prompt(sbq, sbk, sbv).reshape(n_p, SB_WIDTH)
        o_d = _diff_prompt(lamv, sw, dq, dk, dv, lambda_init).reshape(n_p, DIFF_V_WIDTH)

        p_s = _inproj_sample(hs, mods_s[0], mods_s[1], nw_a, w_in_bf).reshape(nseq, n_new, N_PIECES * PIECE)
        caches = [jnp.transpose(c[l], (0, 2, 3, 1)).reshape(n_pool, SB_WIDTH, page)
                  for c in (cache_sb_k, cache_sb_v)]
        caches += [c[l].reshape(n_pool, page * DIFF_HEADS, DIFF_V_DIM) for c in (cache_diff_k, cache_diff_v)]
        o_s = _sample_attention(page_table, lamv, sw,
                                _sb_query_rows(p_s[:, :, 0:PIECE]),
                                _diff_query_rows(p_s[:, :, 3 * PIECE:4 * PIECE]),
                                p_s, caches, lambda_init).reshape(n_s, 2 * PIECE)

        h1_p, f_p, gates_p = _post(hp, o_sb, o_d, mods_p[2], mods_p[3], mods_p[4], nw_f, w_out_bf,
                                   rw_hi, rw_lo, rb, seq)
        h1_s, f_s, gates_s = _post(hs, o_s[:, :SB_WIDTH], o_s[:, SB_WIDTH:], mods_s[2], mods_s[3], mods_s[4],
                                   nw_f, w_out_bf, rw_hi, rw_lo, rb, None)

        f_out = _moe(jnp.concatenate([f_p, f_s], axis=0), jnp.concatenate([gates_p, gates_s], axis=0),
                     w_gate_up[l].astype(BF16), b_gate_up[l].reshape(N_EXPERTS, 1, -1),
                     w_down[l].astype(BF16), b_down[l].reshape(N_EXPERTS, 1, -1))
        last = l == depth - 1
        hp = _final(h1_p, f_out, 0, mods_p[5], final_w, seq, last)
        hs = _final(h1_s, f_out, n_p, mods_s[5], final_w, None, last)

        p_s_flat = p_s.reshape(n_s, N_PIECES * PIECE)
        new_kv[0].append(jnp.transpose(sbk_t.reshape(batch, SB_HEADS, HEAD_DIM, seq), (0, 3, 1, 2)))
        new_kv[1].append(jnp.transpose(sbv_t.reshape(batch, SB_HEADS, HEAD_DIM, seq), (0, 3, 1, 2)))
        new_kv[2].append(dk_f.reshape(batch, seq, DIFF_HEADS, 2 * HEAD_DIM))
        new_kv[3].append(dv_f.reshape(batch, seq, DIFF_HEADS, DIFF_V_DIM))
        new_kv[4].append(p_s_flat[:, 1 * PIECE:2 * PIECE].reshape(nseq, n_new, SB_HEADS, HEAD_DIM))
        new_kv[5].append(p_s_flat[:, 2 * PIECE:3 * PIECE].reshape(nseq, n_new, SB_HEADS, HEAD_DIM))
        new_kv[6].append(p_s_flat[:, 4 * PIECE:5 * PIECE].reshape(nseq, n_new, DIFF_HEADS, 2 * HEAD_DIM))
        new_kv[7].append(p_s_flat[:, 5 * PIECE:6 * PIECE].reshape(nseq, n_new, DIFF_HEADS, DIFF_V_DIM))

    return (hp.reshape(batch, seq, d), hs.reshape(nseq, n_new, d)) + tuple(jnp.stack(v) for v in new_kv)
```

```python
import functools
import math

import jax
import jax.numpy as jnp
from jax import lax
from jax.experimental import pallas as pl
from jax.experimental.pallas import tpu as pltpu

F32 = jnp.float32
BF16 = jnp.bfloat16
I32 = jnp.int32
U32 = jnp.uint32

HEAD_DIM = 64
SB_HEADS = 8
DIFF_HEADS = 4
DIFF_V_DIM = 2 * HEAD_DIM
SB_WIDTH = SB_HEADS * HEAD_DIM
DIFF_QK_WIDTH = DIFF_HEADS * 2 * HEAD_DIM
DIFF_V_WIDTH = DIFF_HEADS * DIFF_V_DIM
PIECE = 512
N_PIECES = 6
N_EXPERTS = 32
TOP_K = 4
SWIGLU_LIMIT = 7.0
SWIGLU_ALPHA = 1.702
RMS_EPS = 1e-5
NEG_BIG = -1e30
QK_SCALE = HEAD_DIM ** -0.5
SB_SKIP_BELOW = -104.0
SB_REACH_MARGIN = 8.0
ALIBI_SLOPES = tuple(2.0 ** (-8.0 * (h + 1) / DIFF_HEADS) for h in range(DIFF_HEADS))

ATTN_BLOCK = 256
DIFF_BLOCK = 512
TOKEN_BLOCK = 512
PAGES_PER_STEP = 8
MOE_MAX_TOKENS = 3584
MOE_CHUNK = 512
MOE_PAD_ROWS = 8
VMEM_LIMIT = 56 * 1024 * 1024


def _cparams(sem, vmem=None):
    return pltpu.CompilerParams(dimension_semantics=sem, vmem_limit_bytes=vmem)


def _dot(a, b):
    return jnp.dot(a, b, preferred_element_type=F32)


def _dot_nt(a, b):
    return lax.dot_general(a, b, (((1,), (1,)), ((), ())), preferred_element_type=F32)


def _split_bf16(x):
    hi = x.astype(BF16)
    lo = (x - hi.astype(F32)).astype(BF16)
    return hi, lo


def _ada_kernel(c_ref, w_ref, b_ref, o_ref):
    c = c_ref[...]
    h = (c * jax.nn.sigmoid(c)).astype(BF16)
    o_ref[...] = _dot(h, w_ref[...].astype(BF16)) + b_ref[...]


def _ada(c, w, b):
    rows, d = c.shape
    n = w.shape[1]
    return pl.pallas_call(
        _ada_kernel,
        grid=(n // d,),
        in_specs=[pl.BlockSpec((rows, d), lambda j: (0, 0)),
                  pl.BlockSpec((d, d), lambda j: (0, j)),
                  pl.BlockSpec((1, d), lambda j: (0, j))],
        out_specs=pl.BlockSpec((rows, d), lambda j: (0, j)),
        out_shape=jax.ShapeDtypeStruct((rows, n), F32),
        compiler_params=_cparams(("arbitrary",)),
        name="ada",
    )(c, w, b.reshape(1, n))


def _norm_mod(x, nw, sh, sc):
    ms = jnp.mean(x * x, axis=-1, keepdims=True)
    y = x * lax.rsqrt(ms + RMS_EPS) * nw
    return y * (1.0 + sc) + sh


def _inproj_prompt_kernel(x_ref, sh_ref, sc_ref, nw_ref, w_ref, wt_ref,
                          sbq_o, sbk_o, sbv_o, dq_o, dk_o, dv_o,
                          sbk_t, sbv_t, dk_f, dv_f):
    u = _norm_mod(x_ref[...], nw_ref[...], sh_ref[0], sc_ref[0]).astype(BF16)
    tm = u.shape[0]
    hm_outs = (sbq_o, sbk_o, sbv_o, dq_o, dk_o, None)
    t_outs = (None, sbk_t, sbv_t, None, None, None)
    f_outs = (None, None, None, None, dk_f, dv_f)
    scales = (QK_SCALE, None, None, QK_SCALE, None, None)
    for piece in range(N_PIECES):
        p = _dot(u, w_ref[:, piece * PIECE:(piece + 1) * PIECE])
        if t_outs[piece] is not None:
            t_outs[piece][0] = _dot_nt(wt_ref[(piece - 1) * PIECE:piece * PIECE, :], u)
        if f_outs[piece] is not None:
            for h in range(DIFF_HEADS):
                f_outs[piece][pl.ds(h, tm, stride=DIFF_HEADS), :] = p[:, h * DIFF_V_DIM:(h + 1) * DIFF_V_DIM]
        if scales[piece] is not None:
            p = p * scales[piece]
        pb = p.astype(BF16)
        if hm_outs[piece] is not None:
            for h in range(PIECE // HEAD_DIM):
                hm_outs[piece][0, h] = pb[:, h * HEAD_DIM:(h + 1) * HEAD_DIM]
        else:
            for h in range(DIFF_HEADS):
                dv_o[0, h] = pb[:, h * DIFF_V_DIM:(h + 1) * DIFF_V_DIM]


def _inproj_prompt(x, sh, sc, nw, w_bf, wt_sbkv, batch, seq):
    n, d = x.shape
    tm = min(TOKEN_BLOCK, seq)
    nb = seq // tm
    hm = lambda i: (i // nb, 0, i % nb, 0)
    row = lambda i: (i, 0)
    mod = lambda i: (i // nb, 0, 0)
    hm64 = jax.ShapeDtypeStruct((batch, 8, seq, HEAD_DIM), BF16)
    hm128 = jax.ShapeDtypeStruct((batch, DIFF_HEADS, seq, DIFF_V_DIM), BF16)
    transposed = jax.ShapeDtypeStruct((batch, PIECE, seq), F32)
    head_rows = jax.ShapeDtypeStruct((n * DIFF_HEADS, DIFF_V_DIM), F32)
    return pl.pallas_call(
        _inproj_prompt_kernel,
        grid=(n // tm,),
        in_specs=[pl.BlockSpec((tm, d), row),
                  pl.BlockSpec((1, 1, d), mod),
                  pl.BlockSpec((1, 1, d), mod),
                  pl.BlockSpec((1, d), lambda i: (0, 0)),
                  pl.BlockSpec((d, N_PIECES * PIECE), lambda i: (0, 0)),
                  pl.BlockSpec((2 * PIECE, d), lambda i: (0, 0))],
        out_specs=[pl.BlockSpec((1, 8, tm, HEAD_DIM), hm)] * 5
                  + [pl.BlockSpec((1, DIFF_HEADS, tm, DIFF_V_DIM), hm)]
                  + [pl.BlockSpec((1, PIECE, tm), lambda i: (i // nb, 0, i % nb))] * 2
                  + [pl.BlockSpec((tm * DIFF_HEADS, DIFF_V_DIM), row)] * 2,
        out_shape=[hm64] * 5 + [hm128] + [transposed] * 2 + [head_rows] * 2,
        compiler_params=_cparams(("arbitrary",), VMEM_LIMIT),
        name="inproj_prompt",
    )(x, sh, sc, nw, w_bf, wt_sbkv)


def _inproj_sample_kernel(x_ref, sh_ref, sc_ref, nw_ref, w_ref, o_ref):
    u = _norm_mod(x_ref[...], nw_ref[...], sh_ref[...], sc_ref[...]).astype(BF16)
    for piece in range(N_PIECES):
        sl = slice(piece * PIECE, (piece + 1) * PIECE)
        o_ref[:, sl] = _dot(u, w_ref[:, sl])


def _inproj_sample(x, sh, sc, nw, w_bf):
    n, d = x.shape
    tm = min(TOKEN_BLOCK, n)
    row = lambda i: (i, 0)
    return pl.pallas_call(
        _inproj_sample_kernel,
        grid=(n // tm,),
        in_specs=[pl.BlockSpec((tm, d), row), pl.BlockSpec((tm, d), row), pl.BlockSpec((tm, d), row),
                  pl.BlockSpec((1, d), lambda i: (0, 0)),
                  pl.BlockSpec((d, N_PIECES * PIECE), lambda i: (0, 0))],
        out_specs=pl.BlockSpec((tm, N_PIECES * PIECE), row),
        out_shape=jax.ShapeDtypeStruct((n, N_PIECES * PIECE), F32),
        compiler_params=_cparams(("arbitrary",), VMEM_LIMIT),
        name="inproj_sample",
    )(x, sh, sc, nw, w_bf)


def _sb_logs(z):
    t = jnp.log1p(jnp.exp(-jnp.abs(z)))
    return jnp.minimum(z, 0.0) - t, -jnp.maximum(z, 0.0) - t


def _strict_upper(t):
    r = lax.broadcasted_iota(I32, (t, t), 0)
    c = lax.broadcasted_iota(I32, (t, t), 1)
    return jnp.where(r > c, 1.0, 0.0).astype(BF16)


def _suffix_sum(ls, tri):
    hi, lo = _split_bf16(ls)
    return _dot(hi, tri) + _dot(lo, tri)


def _sb_prompt_kernel(q_ref, k_ref, v_ref, o_ref, carry_ref, acc_ref, *, blk):
    i = pl.program_id(1)
    tri = _strict_upper(blk)
    r = lax.broadcasted_iota(I32, (blk, blk), 0)
    c = lax.broadcasted_iota(I32, (blk, blk), 1)
    vis = c < r

    def block(j, diag):
        start = pl.multiple_of(j * blk, blk)
        for h in range(SB_HEADS):
            k = k_ref[0, h, pl.ds(start, blk), :]
            v = v_ref[0, h, pl.ds(start, blk), :]
            lb, ls = _sb_logs(_dot_nt(q_ref[0, h], k))
            if diag:
                ls = jnp.where(vis, ls, 0.0)
            carry = carry_ref[h]
            a = jnp.exp(lb + _suffix_sum(ls, tri) + carry)
            if diag:
                a = jnp.where(vis, a, 0.0)
            acc_ref[h] = acc_ref[h] + _dot(a.astype(BF16), v)
            carry_ref[h] = carry + jnp.sum(ls, axis=1, keepdims=True)

    carry_ref[...] = jnp.zeros_like(carry_ref)
    acc_ref[...] = jnp.zeros_like(acc_ref)
    block(i, True)

    def cond(s):
        return jnp.logical_and(s[0] >= 0, s[1] > SB_SKIP_BELOW)

    def body(s):
        block(s[0], False)
        return s[0] - 1, jnp.max(carry_ref[...])

    lax.while_loop(cond, body, (i - 1, jnp.max(carry_ref[...])))
    o_ref[0] = jnp.concatenate([acc_ref[h] for h in range(SB_HEADS)], axis=1).astype(o_ref.dtype)


def _sb_prompt(q, k, v):
    b, h, s, dh = q.shape
    blk = min(ATTN_BLOCK, s)
    return pl.pallas_call(
        functools.partial(_sb_prompt_kernel, blk=blk),
        grid=(b, s // blk),
        in_specs=[pl.BlockSpec((1, h, blk, dh), lambda bi, i: (bi, 0, i, 0)),
                  pl.BlockSpec((1, h, s, dh), lambda bi, i: (bi, 0, 0, 0)),
                  pl.BlockSpec((1, h, s, dh), lambda bi, i: (bi, 0, 0, 0))],
        out_specs=pl.BlockSpec((1, blk, h * dh), lambda bi, i: (bi, i, 0)),
        out_shape=jax.ShapeDtypeStruct((b, s, h * dh), BF16),
        scratch_shapes=[pltpu.VMEM((h, blk, 1), F32), pltpu.VMEM((h, blk, dh), F32)],
        compiler_params=_cparams(("arbitrary", "arbitrary"), VMEM_LIMIT),
        name="sb_prompt",
    )(q, k, v)


def _lambda_from(lamv_ref, lambda_init):
    l1 = jnp.sum(lamv_ref[0:1, :] * lamv_ref[1:2, :], axis=1, keepdims=True)
    l2 = jnp.sum(lamv_ref[2:3, :] * lamv_ref[3:4, :], axis=1, keepdims=True)
    return jnp.exp(l1) - jnp.exp(l2) + lambda_init


def _head_slope(h):
    out = jnp.float32(ALIBI_SLOPES[-1])
    for hh in range(DIFF_HEADS - 2, -1, -1):
        out = jnp.where(h == hh, jnp.float32(ALIBI_SLOPES[hh]), out)
    return out


def _diff_prompt_kernel(lamv_ref, sw_ref, q_ref, k_ref, v_ref, o_ref, *, blk, lambda_init):
    h = pl.program_id(1)
    i = pl.program_id(2)
    slope = _head_slope(h)
    r = lax.broadcasted_iota(I32, (blk, blk), 0)
    c = lax.broadcasted_iota(I32, (blk, blk), 1)
    srel = slope * (r - c).astype(F32)
    causal = c <= r
    qs = (q_ref[0, 0], q_ref[0, 1])

    def block(j, st, diag):
        start = pl.multiple_of(j * blk, blk)
        v = v_ref[0, 0, pl.ds(start, blk), :]
        off = slope * ((i - j) * blk).astype(F32)
        new = []
        for comp in range(2):
            m, l, acc = st[3 * comp:3 * comp + 3]
            k = k_ref[0, comp, pl.ds(start, blk), :]
            s = _dot_nt(qs[comp], k) - srel
            if diag:
                s = jnp.where(causal, s, NEG_BIG)
            m_new = jnp.maximum(m, jnp.max(s, axis=1, keepdims=True) - off)
            p = jnp.exp(s - (m_new + off))
            alpha = jnp.exp(m - m_new)
            l = l * alpha + jnp.sum(p, axis=1, keepdims=True)
            acc = acc * alpha + _dot(p.astype(BF16), v)
            new += [m_new, l, acc]
        return tuple(new)

    init = (jnp.full((blk, 1), NEG_BIG, F32), jnp.zeros((blk, 1), F32),
            jnp.zeros((blk, DIFF_V_DIM), F32)) * 2
    st = lax.fori_loop(0, i, lambda j, st: block(j, st, False), init)
    st = block(i, st, True)
    lam = _lambda_from(lamv_ref, lambda_init)
    o = st[2] / st[1] - lam * (st[5] / st[4])
    o = o * lax.rsqrt(jnp.mean(o * o, axis=1, keepdims=True) + RMS_EPS)
    o_ref[0] = (o * sw_ref[...] * (1.0 - lambda_init)).astype(o_ref.dtype)


def _diff_prompt(lamv, sw, q, k, v, lambda_init):
    b, h2, s, dh = q.shape
    blk = min(DIFF_BLOCK, s)
    return pl.pallas_call(
        functools.partial(_diff_prompt_kernel, blk=blk, lambda_init=lambda_init),
        grid=(b, DIFF_HEADS, s // blk),
        in_specs=[pl.BlockSpec((4, HEAD_DIM), lambda bi, hi, i: (0, 0)),
                  pl.BlockSpec((1, DIFF_V_DIM), lambda bi, hi, i: (0, 0)),
                  pl.BlockSpec((1, 2, blk, dh), lambda bi, hi, i: (bi, hi, i, 0)),
                  pl.BlockSpec((1, 2, s, dh), lambda bi, hi, i: (bi, hi, 0, 0)),
                  pl.BlockSpec((1, 1, s, DIFF_V_DIM), lambda bi, hi, i: (bi, hi, 0, 0))],
        out_specs=pl.BlockSpec((1, blk, DIFF_V_DIM), lambda bi, hi, i: (bi, i, hi)),
        out_shape=jax.ShapeDtypeStruct((b, s, DIFF_V_WIDTH), BF16),
        compiler_params=_cparams(("arbitrary", "arbitrary", "arbitrary"), VMEM_LIMIT),
        name="diff_prompt",
    )(lamv, sw, q, k, v)


def _sample_kernel(pt_ref, sb_steps_ref, lamv_ref, sw_ref, qsb_ref, qd_ref, new_ref, *rest,
                   n_new, page, n_pages, pages_per_step, lambda_init):
    g_ = pages_per_step
    sbk_pages = rest[0 * g_:1 * g_]
    sbv_pages = rest[1 * g_:2 * g_]
    dk_pages = rest[2 * g_:3 * g_]
    dv_pages = rest[3 * g_:4 * g_]
    o_ref = rest[4 * g_]
    sb_carry, sb_acc, d_m, d_l, d_acc = rest[4 * g_ + 1:]
    t = pl.program_id(1)
    rows = 8 * n_new
    hrows = 2 * n_new
    past_len = n_pages * page

    ridx = lax.broadcasted_iota(I32, (rows, 1), 0)
    qidx = ridx % n_new
    grp = ridx // n_new
    slope_row = jnp.zeros((rows, 1), F32)
    for hh in range(DIFF_HEADS):
        slope_row = jnp.where(grp // 2 == hh, jnp.float32(ALIBI_SLOPES[hh]), slope_row)
    qpos = (past_len + qidx).astype(F32)

    @pl.when(t == 0)
    def _():
        qsb = qsb_ref[0].astype(F32)
        col = lax.broadcasted_iota(I32, (rows, DIFF_QK_WIDTH), 1)
        qd = jnp.where(col // DIFF_V_DIM == grp // 2,
                       jnp.concatenate([qd_ref[0].astype(F32)] * DIFF_HEADS, axis=1), 0.0)
        carry = jnp.zeros((rows, 1), F32)
        acc = jnp.zeros((rows, SB_WIDTH), F32)
        m = jnp.full((rows, 1), NEG_BIG, F32)
        l = jnp.zeros((rows, 1), F32)
        dacc = jnp.zeros((rows, DIFF_V_DIM), F32)
        for j in range(n_new - 1, -1, -1):
            sbk = new_ref[0, j:j + 1, 1 * PIECE:2 * PIECE]
            sbv = new_ref[0, j:j + 1, 2 * PIECE:3 * PIECE]
            dk = new_ref[0, j:j + 1, 4 * PIECE:5 * PIECE]
            dv = new_ref[0, j:j + 1, 5 * PIECE:6 * PIECE]
            z = jnp.sum(qsb * sbk, axis=1, keepdims=True)
            lb, ls = _sb_logs(z)
            vis = j < qidx
            ls = jnp.where(vis, ls, 0.0)
            a = jnp.where(vis, jnp.exp(lb + carry), 0.0)
            acc = acc + a * sbv
            carry = carry + ls
            s = jnp.sum(qd * dk, axis=1, keepdims=True) - slope_row * (qidx - j).astype(F32)
            s = jnp.where(j <= qidx, s, NEG_BIG)
            m_new = jnp.maximum(m, s)
            p = jnp.exp(s - m_new)
            alpha = jnp.exp(m - m_new)
            l = l * alpha + p
            dv_rows = jnp.concatenate(
                [jnp.broadcast_to(dv[:, hh * DIFF_V_DIM:(hh + 1) * DIFF_V_DIM], (hrows, DIFF_V_DIM))
                 for hh in range(DIFF_HEADS)], axis=0)
            dacc = dacc * alpha + p * dv_rows
            m = m_new
        sb_carry[...] = carry
        sb_acc[...] = acc
        d_m[...] = m
        d_l[...] = l
        d_acc[...] = dacc

    @pl.when(jnp.logical_and(t < sb_steps_ref[pl.program_id(0)], jnp.max(sb_carry[...]) > SB_SKIP_BELOW))
    def _():
        qsb = qsb_ref[0]
        tri = _strict_upper(page)
        z = jnp.concatenate([_dot(qsb, sbk_pages[g][0].astype(BF16)) for g in range(g_)], axis=0)
        lb, ls = _sb_logs(z)
        suffix = _suffix_sum(ls, tri)
        rowsum = jnp.sum(ls, axis=1, keepdims=True)
        carry = sb_carry[...]
        carries = []
        for g in range(g_):
            carries.append(carry)
            carry = carry + rowsum[g * rows:(g + 1) * rows]
        a = jnp.exp(lb + suffix + jnp.concatenate(carries, axis=0)).astype(BF16)
        acc = sb_acc[...]
        for g in range(g_):
            acc = acc + _dot_nt(a[g * rows:(g + 1) * rows], sbv_pages[g][0].astype(BF16))
        sb_acc[...] = acc
        sb_carry[...] = carry

    qd = qd_ref[0]
    lane = lax.broadcasted_iota(I32, (1, page), 1).astype(F32)
    ss, bases = [], []
    for g in range(g_):
        pg = n_pages - 1 - (t * g_ + g)
        for hh in range(DIFF_HEADS):
            kh = dk_pages[g][0, pl.ds(hh, page, stride=DIFF_HEADS), :].astype(BF16)
            ss.append(_dot_nt(qd[hh * hrows:(hh + 1) * hrows], kh))
        bases.append(qpos - (pg * page).astype(F32))
    s = jnp.concatenate(ss, axis=0)
    base = jnp.concatenate(bases, axis=0)
    slope_all = jnp.concatenate([slope_row] * g_, axis=0)
    s = s - slope_all * (base - lane)
    rowmax = jnp.max(s, axis=1, keepdims=True)
    m_old = d_m[...]
    m_new = m_old
    for g in range(g_):
        m_new = jnp.maximum(m_new, rowmax[g * rows:(g + 1) * rows])
    p = jnp.exp(s - jnp.concatenate([m_new] * g_, axis=0))
    rowsum = jnp.sum(p, axis=1, keepdims=True)
    alpha = jnp.exp(m_old - m_new)
    l = d_l[...] * alpha
    for g in range(g_):
        l = l + rowsum[g * rows:(g + 1) * rows]
    pb = p.astype(BF16)
    pvs = []
    for hh in range(DIFF_HEADS):
        pv = jnp.zeros((hrows, DIFF_V_DIM), F32)
        for g in range(g_):
            vh = dv_pages[g][0, pl.ds(hh, page, stride=DIFF_HEADS), :].astype(BF16)
            pv = pv + _dot(pb[g * rows + hh * hrows:g * rows + (hh + 1) * hrows], vh)
        pvs.append(pv)
    d_acc[...] = d_acc[...] * alpha + jnp.concatenate(pvs, axis=0)
    d_l[...] = l
    d_m[...] = m_new

    @pl.when(t == pl.num_programs(1) - 1)
    def _():
        col = lax.broadcasted_iota(I32, (rows, SB_WIDTH), 1)
        qrow = lax.broadcasted_iota(I32, (8, rows), 0)
        rcol = lax.broadcasted_iota(I32, (8, rows), 1)
        sel = jnp.where(rcol % n_new == qrow, 1.0, 0.0).astype(BF16)
        hi, lo = _split_bf16(jnp.where(col // HEAD_DIM == grp, sb_acc[...], 0.0))
        o_sb = _dot(sel, hi) + _dot(sel, lo)

        lam = _lambda_from(lamv_ref, lambda_init)
        on = d_acc[...] / d_l[...]
        segs = []
        for hh in range(DIFF_HEADS):
            both = on[hh * hrows:(hh + 1) * hrows]
            seg = both - lam * pltpu.roll(both, n_new, axis=0)
            seg = seg * lax.rsqrt(jnp.mean(seg * seg, axis=1, keepdims=True) + RMS_EPS)
            segs.append(seg * sw_ref[...] * (1.0 - lambda_init))
        full = jnp.concatenate([o_sb] + segs, axis=1)
        o_ref[0] = full[:n_new].astype(o_ref.dtype)


def _sb_reach_kernel(pt_ref, qsb_ref, new_ref, *rest, n_new, pages_per_step):
    k_pages = rest[:pages_per_step]
    o_ref = rest[pages_per_step]
    rows = 8 * n_new
    qidx = lax.broadcasted_iota(I32, (rows, 1), 0) % n_new
    qsb = qsb_ref[0]
    qsb_f = qsb.astype(F32)
    carry = jnp.zeros((rows, 1), F32)
    for j in range(n_new):
        z = jnp.sum(qsb_f * new_ref[0, j:j + 1, 1 * PIECE:2 * PIECE], axis=1, keepdims=True)
        carry = carry + jnp.where(j < qidx, _sb_logs(z)[1], 0.0)
    for g in range(pages_per_step):
        ls = _sb_logs(_dot(qsb, k_pages[g][0].astype(BF16)))[1]
        carry = carry + jnp.sum(ls, axis=1, keepdims=True)
    settled = jnp.max(carry, axis=0, keepdims=True) < SB_SKIP_BELOW - SB_REACH_MARGIN
    o_ref[0] = jnp.broadcast_to(jnp.where(settled, 1, 0).astype(I32), o_ref.shape[1:])


def _sb_reach(page_table, qsb_bd, p_new, cache_k, g_):
    nseq, n_pages = page_table.shape
    n_new = p_new.shape[1]
    rows = 8 * n_new
    rows_per_page, page = cache_k.shape[1], cache_k.shape[2]
    specs = [pl.BlockSpec((1, rows_per_page, page), lambda b, pt, g=g: (pt[b, n_pages - 1 - g], 0, 0))
             for g in range(g_)]
    grid_spec = pltpu.PrefetchScalarGridSpec(
        num_scalar_prefetch=1,
        grid=(nseq,),
        in_specs=[pl.BlockSpec((1, rows, SB_WIDTH), lambda b, pt: (b, 0, 0)),
                  pl.BlockSpec((1, n_new, N_PIECES * PIECE), lambda b, pt: (b, 0, 0))] + specs,
        out_specs=pl.BlockSpec((1, 8, 128), lambda b, pt: (b, 0, 0)),
    )
    settled = pl.pallas_call(
        functools.partial(_sb_reach_kernel, n_new=n_new, pages_per_step=g_),
        grid_spec=grid_spec,
        out_shape=jax.ShapeDtypeStruct((nseq, 8, 128), I32),
        compiler_params=_cparams(("arbitrary",), VMEM_LIMIT),
        name="sb_reach",
    )(page_table, qsb_bd, p_new, *([cache_k] * g_))
    return jnp.where(settled[:, 0, 0] == 1, 1, n_pages // g_).astype(I32)


def _sample_attention(page_table, lamv, sw, qsb_bd, qd_bd, p_new, caches, lambda_init):
    nseq, n_pages = page_table.shape
    n_new = p_new.shape[1]
    assert 2 * n_new == 8, "differential rows per head must fill one sublane group"
    rows_per_page = caches[0].shape[1]
    page = caches[0].shape[2]
    g_ = math.gcd(PAGES_PER_STEP, n_pages)
    rows = 8 * n_new
    sb_steps = _sb_reach(page_table, qsb_bd, p_new, caches[0], g_)

    def page_spec(g, sb):
        if sb:
            return pl.BlockSpec(
                (1, rows_per_page, page),
                lambda b, t, pt, ns, g=g: (pt[b, n_pages - 1 - (jnp.minimum(t, ns[b] - 1) * g_ + g)], 0, 0))
        return pl.BlockSpec((1, rows_per_page, page),
                            lambda b, t, pt, ns, g=g: (pt[b, n_pages - 1 - (t * g_ + g)], 0, 0))

    cache_specs, cache_args = [], []
    for ci, cache in enumerate(caches):
        for g in range(g_):
            cache_specs.append(page_spec(g, ci < 2))
            cache_args.append(cache)
    grid_spec = pltpu.PrefetchScalarGridSpec(
        num_scalar_prefetch=2,
        grid=(nseq, n_pages // g_),
        in_specs=[pl.BlockSpec((4, HEAD_DIM), lambda b, t, pt, ns: (0, 0)),
                  pl.BlockSpec((1, DIFF_V_DIM), lambda b, t, pt, ns: (0, 0)),
                  pl.BlockSpec((1, rows, SB_WIDTH), lambda b, t, pt, ns: (b, 0, 0)),
                  pl.BlockSpec((1, rows, DIFF_V_DIM), lambda b, t, pt, ns: (b, 0, 0)),
                  pl.BlockSpec((1, n_new, N_PIECES * PIECE), lambda b, t, pt, ns: (b, 0, 0))] + cache_specs,
        out_specs=pl.BlockSpec((1, n_new, 2 * PIECE), lambda b, t, pt, ns: (b, 0, 0)),
        scratch_shapes=[pltpu.VMEM((rows, 1), F32), pltpu.VMEM((rows, SB_WIDTH), F32),
                        pltpu.VMEM((rows, 1), F32), pltpu.VMEM((rows, 1), F32),
                        pltpu.VMEM((rows, DIFF_V_DIM), F32)],
    )
    return pl.pallas_call(
        functools.partial(_sample_kernel, n_new=n_new, page=page, n_pages=n_pages,
                          pages_per_step=g_, lambda_init=lambda_init),
        grid_spec=grid_spec,
        out_shape=jax.ShapeDtypeStruct((nseq, n_new, 2 * PIECE), BF16),
        compiler_params=_cparams(("arbitrary", "arbitrary"), VMEM_LIMIT),
        name="sample_attention",
    )(page_table, sb_steps, lamv, sw, qsb_bd, qd_bd, p_new, *cache_args)


def _sb_query_rows(q):
    nseq, n_new, w = q.shape
    tiled = jnp.broadcast_to(q[:, None], (nseq, 8, n_new, w)).reshape(nseq, 8 * n_new, w)
    grp = (jnp.arange(8 * n_new) // n_new)[:, None]
    col = (jnp.arange(w) // HEAD_DIM)[None, :]
    return jnp.where(grp == col, tiled * QK_SCALE, 0.0).astype(BF16)


def _diff_query_rows(q):
    nseq, n_new, _ = q.shape
    q5 = q.reshape(nseq, n_new, DIFF_HEADS, 2, HEAD_DIM) * QK_SCALE
    q5 = jnp.transpose(q5, (0, 2, 3, 1, 4))
    eye = jnp.eye(2, dtype=q.dtype)[None, None, :, None, :, None]
    out = q5[:, :, :, :, None, :] * eye
    return out.reshape(nseq, 8 * n_new, DIFF_V_DIM).astype(BF16)


def _post_kernel(x_ref, osb_ref, od_ref, g_ref, sh_ref, sc_ref, nw_ref, wo_ref, rw_hi_ref, rw_lo_ref,
                 rb_ref, h_ref, f_ref, gates_ref, *, per_token):
    take = (lambda r: r[...]) if per_token else (lambda r: r[0])
    proj = _dot(osb_ref[...], wo_ref[:SB_WIDTH, :]) + _dot(od_ref[...], wo_ref[SB_WIDTH:, :])
    h = x_ref[...] + take(g_ref) * proj
    h_ref[...] = h
    f = _norm_mod(h, nw_ref[...], take(sh_ref), take(sc_ref))

    f_ref[...] = f

    f_hi, f_lo = _split_bf16(f)
    logits = (_dot(f_hi, rw_hi_ref[...]) + _dot(f_lo, rw_hi_ref[...]) + _dot(f_hi, rw_lo_ref[...])
              + rb_ref[...])
    lane = lax.broadcasted_iota(I32, logits.shape, 1).astype(F32)
    work = logits
    firsts, vals = [], []
    for _ in range(TOP_K):
        mx = jnp.max(work, axis=1, keepdims=True)
        first = jnp.min(jnp.where(work == mx, lane, float(N_EXPERTS)), axis=1, keepdims=True)
        firsts.append(first)
        vals.append(mx)
        work = jnp.where(lane == first, -jnp.inf, work)
    es = [jnp.exp(v - vals[0]) for v in vals]
    denom = es[0] + es[1] + es[2] + es[3]
    slot = lax.broadcasted_iota(I32, gates_ref.shape, 1)
    route = jnp.zeros(gates_ref.shape, F32)
    for k in range(TOP_K):
        route = jnp.where(slot == k, firsts[k], route)
        route = jnp.where(slot == TOP_K + k, es[k] / denom, route)
    gates_ref[...] = route


def _post(x, osb, od, g, sh, sc, nw, wo_bf, rw_hi, rw_lo, rb, tokens_per_group):
    n, d = x.shape
    per_token = tokens_per_group is None
    tm = min(TOKEN_BLOCK, n if per_token else tokens_per_group)
    row = lambda i: (i, 0)
    const = lambda i: (0, 0)
    if per_token:
        mod_spec = pl.BlockSpec((tm, d), row)
    else:
        nb = tokens_per_group // tm
        mod_spec = pl.BlockSpec((1, 1, d), lambda i: (i // nb, 0, 0))
    return pl.pallas_call(
        functools.partial(_post_kernel, per_token=per_token),
        grid=(n // tm,),
        in_specs=[pl.BlockSpec((tm, d), row),
                  pl.BlockSpec((tm, SB_WIDTH), row),
                  pl.BlockSpec((tm, DIFF_V_WIDTH), row),
                  mod_spec, mod_spec, mod_spec,
                  pl.BlockSpec((1, d), const),
                  pl.BlockSpec((SB_WIDTH + DIFF_V_WIDTH, d), const),
                  pl.BlockSpec((d, N_EXPERTS), const),
                  pl.BlockSpec((d, N_EXPERTS), const),
                  pl.BlockSpec((1, N_EXPERTS), const)],
        out_specs=[pl.BlockSpec((tm, d), row),
                   pl.BlockSpec((tm, d), row),
                   pl.BlockSpec((tm, 2 * TOP_K), row)],
        out_shape=[jax.ShapeDtypeStruct((n, d), F32),
                   jax.ShapeDtypeStruct((n, d), F32),
                   jax.ShapeDtypeStruct((n, 2 * TOP_K), F32)],
        compiler_params=_cparams(("arbitrary",), VMEM_LIMIT),
        name="post_attention",
    )(x, osb, od, g, sh, sc, nw, wo_bf, rw_hi, rw_lo, rb)


def _moe_plan(n_tokens):
    nblk = 1
    while n_tokens % nblk or n_tokens // nblk > MOE_MAX_TOKENS or (n_tokens // nblk) % 8:
        nblk += 1
    ttok = n_tokens // nblk
    mc = min(MOE_CHUNK, ttok)
    smax = N_EXPERTS + (ttok * TOP_K) // mc
    return nblk, ttok, mc, smax


def _routing(gates, nblk, ttok, mc, smax):
    n = gates.shape[0]
    per_blk = ttok * TOP_K
    tpad = 1 << (ttok - 1).bit_length()
    topi = gates[:, :TOP_K].astype(I32)
    keys = (topi * tpad + (jnp.arange(n, dtype=I32) % ttok)[:, None]).reshape(nblk, per_blk)
    skeys, sw = lax.sort((keys, gates[:, TOP_K:].reshape(nblk, per_blk)), dimension=1, num_keys=1)
    bounds = jnp.arange(N_EXPERTS + 1, dtype=I32) * tpad
    first = jnp.sum((skeys[:, :, None] < bounds[None, None, :]).astype(I32), axis=1)
    cnt = first[:, 1:] - first[:, :-1]
    nch = (cnt + mc - 1) // mc
    ends = jnp.cumsum(nch, axis=1)
    starts = ends - nch
    total = ends[:, -1]
    s_ar = jnp.arange(smax, dtype=I32)
    slot_e = jnp.sum((s_ar[None, :, None] >= ends[:, None, :]).astype(I32), axis=-1)
    valid = s_ar[None, :] < total[:, None]
    last_e = jnp.take_along_axis(slot_e, jnp.maximum(total - 1, 0)[:, None], axis=1)
    slot_e = jnp.where(valid, jnp.minimum(slot_e, N_EXPERTS - 1), jnp.minimum(last_e, N_EXPERTS - 1))
    chunk = s_ar[None, :] - jnp.take_along_axis(starts, slot_e, axis=1)
    slot_cnt = jnp.where(valid, jnp.clip(jnp.take_along_axis(cnt, slot_e, axis=1) - chunk * mc, 0, mc), 0)
    lane = jnp.arange(mc, dtype=I32)
    src = jnp.take_along_axis(first[:, :-1], slot_e, axis=1)[:, :, None] + chunk[:, :, None] * mc + lane[None, None, :]
    src = jnp.clip(src, 0, per_blk - 1).reshape(nblk, smax * mc)
    live = lane[None, None, :] < slot_cnt[:, :, None]
    tok = (jnp.take_along_axis(skeys, src, axis=1) & (tpad - 1)).reshape(nblk, smax, mc)
    gw = jnp.where(live, jnp.take_along_axis(sw, src, axis=1).reshape(nblk, smax, mc), 0.0)
    tok = jnp.where(live, tok, ttok + lane[None, None, :] % MOE_PAD_ROWS)
    return slot_e.reshape(-1), slot_cnt.reshape(-1).astype(I32), tok, gw


def _moe_kernel(se_ref, sc_ref, x_hbm, idx_ref, gw_ref, wgu_ref, bgu_ref, wd_ref, bd_ref, out_hbm,
                x_ref, acc_ref, xg_ref, y_ref, sem, *, ttok, mc, smax, d_ff):
    tb = pl.program_id(0)
    s = pl.program_id(1)
    cnt = sc_ref[tb * smax + s]

    @pl.when(jnp.logical_and(tb == 0, s == 0))
    def _():
        xg_ref[...] = jnp.zeros_like(xg_ref)

    @pl.when(s == 0)
    def _():
        cp = pltpu.make_async_copy(x_hbm.at[pl.ds(tb * ttok, ttok), :], x_ref.at[pl.ds(0, ttok), :], sem.at[0])
        cp.start()
        x_ref[pl.ds(ttok, MOE_PAD_ROWS), :] = jnp.zeros((MOE_PAD_ROWS, x_ref.shape[1]), F32)
        acc_ref[...] = jnp.zeros_like(acc_ref)
        cp.wait()

    @pl.when(cnt > 0)
    def _():
        ngrp = (cnt + MOE_PAD_ROWS - 1) // MOE_PAD_ROWS

        def gather(gi, carry):
            for u in range(MOE_PAD_ROWS):
                r = gi * MOE_PAD_ROWS + u
                xg_ref[pl.ds(r, 1), :] = x_ref[pl.ds(idx_ref[0, 0, r], 1), :]
            return carry

        lax.fori_loop(0, ngrp, gather, 0)
        gu = _dot(xg_ref[...].astype(BF16), wgu_ref[0]) + bgu_ref[0]
        g = jnp.minimum(gu[:, :d_ff], SWIGLU_LIMIT)
        u = jnp.clip(gu[:, d_ff:], -SWIGLU_LIMIT, SWIGLU_LIMIT)
        act = ((u + 1.0) * g * jax.nn.sigmoid(SWIGLU_ALPHA * g)).astype(BF16)
        y_ref[...] = _dot(act, wd_ref[0]) + bd_ref[0]

        def scatter(gi, carry):
            for half in range(2):
                rs = [gi * MOE_PAD_ROWS + half * 4 + u_ for u_ in range(4)]
                toks = [idx_ref[0, 0, r] for r in rs]
                new = [acc_ref[pl.ds(t_, 1), :] + gw_ref[0, 0, r] * y_ref[pl.ds(r, 1), :]
                       for t_, r in zip(toks, rs)]
                for t_, v_ in zip(toks, new):
                    acc_ref[pl.ds(t_, 1), :] = v_
            return carry

        lax.fori_loop(0, ngrp, scatter, 0)

    @pl.when(s == smax - 1)
    def _():
        cp = pltpu.make_async_copy(acc_ref.at[pl.ds(0, ttok), :], out_hbm.at[pl.ds(tb * ttok, ttok), :],
                                   sem.at[1])
        cp.start()
        cp.wait()


def _moe(x, gates, wgu_bf, bgu, wd_bf, bd):
    n, d = x.shape
    d_ff = wd_bf.shape[1]
    nblk, ttok, mc, smax = _moe_plan(n)
    slot_e, slot_cnt, tok, gw = _routing(gates, nblk, ttok, mc, smax)
    slot = lambda tb, s, se, sc: (tb * smax + s, 0, 0)
    expert = lambda tb, s, se, sc: (se[tb * smax + s], 0, 0)
    grid_spec = pltpu.PrefetchScalarGridSpec(
        num_scalar_prefetch=2,
        grid=(nblk, smax),
        in_specs=[pl.BlockSpec(memory_space=pl.ANY),
                  pl.BlockSpec((1, 1, mc), slot, memory_space=pltpu.SMEM),
                  pl.BlockSpec((1, 1, mc), slot, memory_space=pltpu.SMEM),
                  pl.BlockSpec((1, d, 2 * d_ff), expert),
                  pl.BlockSpec((1, 1, 2 * d_ff), expert),
                  pl.BlockSpec((1, d_ff, d), expert),
                  pl.BlockSpec((1, 1, d), expert)],
        out_specs=pl.BlockSpec(memory_space=pl.ANY),
        scratch_shapes=[pltpu.VMEM((ttok + MOE_PAD_ROWS, d), F32), pltpu.VMEM((ttok + MOE_PAD_ROWS, d), F32),
                        pltpu.VMEM((mc, d), F32),
                        pltpu.VMEM((mc, d), F32), pltpu.SemaphoreType.DMA((2,))],
    )
    return pl.pallas_call(
        functools.partial(_moe_kernel, ttok=ttok, mc=mc, smax=smax, d_ff=d_ff),
        grid_spec=grid_spec,
        out_shape=jax.ShapeDtypeStruct((n, d), F32),
        compiler_params=_cparams(("arbitrary", "arbitrary"), VMEM_LIMIT),
        name="moe",
    )(slot_e, slot_cnt, x, tok.reshape(nblk * smax, 1, mc), gw.reshape(nblk * smax, 1, mc),
      wgu_bf, bgu, wd_bf, bd)


def _final_kernel(h_ref, f_ref, g_ref, nw_ref, o_ref, *, per_token, last):
    g = g_ref[...] if per_token else g_ref[0]
    h = h_ref[...] + g * f_ref[...]
    if last:
        h = h * lax.rsqrt(jnp.mean(h * h, axis=-1, keepdims=True) + RMS_EPS) * nw_ref[...]
    o_ref[...] = h


def _final(h, f_all, row0, g, nw, tokens_per_group, last):
    n, d = h.shape
    per_token = tokens_per_group is None
    tm = min(TOKEN_BLOCK, n if per_token else tokens_per_group)
    assert row0 % tm == 0
    b0 = row0 // tm
    row = lambda i: (i, 0)
    if per_token:
        g_spec = pl.BlockSpec((tm, d), row)
    else:
        nb = tokens_per_group // tm
        g_spec = pl.BlockSpec((1, 1, d), lambda i: (i // nb, 0, 0))
    return pl.pallas_call(
        functools.partial(_final_kernel, per_token=per_token, last=last),
        grid=(n // tm,),
        in_specs=[pl.BlockSpec((tm, d), row),
                  pl.BlockSpec((tm, d), lambda i: (b0 + i, 0)),
                  g_spec,
                  pl.BlockSpec((1, d), lambda i: (0, 0))],
        out_specs=pl.BlockSpec((tm, d), row),
        out_shape=jax.ShapeDtypeStruct((n, d), F32),
        compiler_params=_cparams(("arbitrary",), VMEM_LIMIT),
        name="final",
    )(h, f_all, g, nw)


def kernel(x_prompt, x_sample, cache_sb_k, cache_sb_v, cache_diff_k, cache_diff_v, page_table, c_prompt, c_sample, ada_w, ada_b, norm_attn_w, w_in, diff_lambda_q1, diff_lambda_k1, diff_lambda_q2, diff_lambda_k2, diff_subln_w, w_out, norm_ffn_w, router_w, router_b, w_gate_up, b_gate_up, w_down, b_down, final_norm_w):
    batch, seq, d = x_prompt.shape
    nseq, n_new, _ = x_sample.shape
    depth = ada_w.shape[0]
    n_p, n_s = batch * seq, nseq * n_new
    n_pool, page = cache_sb_k.shape[1], cache_sb_k.shape[2]

    hp = x_prompt.reshape(n_p, d)
    hs = x_sample.reshape(n_s, d)
    c_all = jnp.concatenate([c_prompt, c_sample], axis=0)
    new_kv = [[] for _ in range(8)]
    final_w = final_norm_w.reshape(1, d)

    for l in range(depth):
        lambda_init = 0.8 - 0.6 * math.exp(-0.3 * l)
        ada = _ada(c_all, ada_w[l], ada_b[l])
        mods_p = [ada[:batch, i * d:(i + 1) * d].reshape(batch, 1, d) for i in range(6)]
        mods_s = [jnp.repeat(ada[batch:, i * d:(i + 1) * d], n_new, axis=0) for i in range(6)]
        lamv = jnp.stack([diff_lambda_q1[l], diff_lambda_k1[l], diff_lambda_q2[l], diff_lambda_k2[l]]).astype(F32)
        sw = diff_subln_w[l].reshape(1, DIFF_V_DIM).astype(F32)
        nw_a = norm_attn_w[l].reshape(1, d)
        nw_f = norm_ffn_w[l].reshape(1, d)
        w_in_bf = w_in[l].astype(BF16)
        w_out_bf = w_out[l].astype(BF16)
        rw_hi = router_w[l].astype(BF16)
        rw_lo = (router_w[l] - rw_hi.astype(F32)).astype(BF16)
        rb = router_b[l].reshape(1, N_EXPERTS)

        (sbq, sbk, sbv, dq, dk, dv, sbk_t, sbv_t, dk_f, dv_f) = _inproj_prompt(
            hp, mods_p[0], mods_p[1], nw_a, w_in_bf, w_in_bf[:, PIECE:3 * PIECE].T, batch, seq)
        o_sb = _sb_prompt(sbq, sbk, sbv).reshape(n_p, SB_WIDTH)
        o_d = _diff_prompt(lamv, sw, dq, dk, dv, lambda_init).reshape(n_p, DIFF_V_WIDTH)

        p_s = _inproj_sample(hs, mods_s[0], mods_s[1], nw_a, w_in_bf).reshape(nseq, n_new, N_PIECES * PIECE)
        caches = [jnp.transpose(c[l], (0, 2, 3, 1)).reshape(n_pool, SB_WIDTH, page)
                  for c in (cache_sb_k, cache_sb_v)]
        caches += [c[l].reshape(n_pool, page * DIFF_HEADS, DIFF_V_DIM) for c in (cache_diff_k, cache_diff_v)]
        o_s = _sample_attention(page_table, lamv, sw,
                                _sb_query_rows(p_s[:, :, 0:PIECE]),
                                _diff_query_rows(p_s[:, :, 3 * PIECE:4 * PIECE]),
                                p_s, caches, lambda_init).reshape(n_s, 2 * PIECE)

        h1_p, f_p, gates_p = _post(hp, o_sb, o_d, mods_p[2], mods_p[3], mods_p[4], nw_f, w_out_bf,
                                   rw_hi, rw_lo, rb, seq)
        h1_s, f_s, gates_s = _post(hs, o_s[:, :SB_WIDTH], o_s[:, SB_WIDTH:], mods_s[2], mods_s[3], mods_s[4],
                                   nw_f, w_out_bf, rw_hi, rw_lo, rb, None)

        f_out = _moe(jnp.concatenate([f_p, f_s], axis=0), jnp.concatenate([gates_p, gates_s], axis=0),
                     w_gate_up[l].astype(BF16), b_gate_up[l].reshape(N_EXPERTS, 1, -1),
                     w_down[l].astype(BF16), b_down[l].reshape(N_EXPERTS, 1, -1))
        last = l == depth - 1
        hp = _final(h1_p, f_out, 0, mods_p[5], final_w, seq, last)
        hs = _final(h1_s, f_out, n_p, mods_s[5], final_w, None, last)

        p_s_flat = p_s.reshape(n_s, N_PIECES * PIECE)
        new_kv[0].append(jnp.transpose(sbk_t.reshape(batch, SB_HEADS, HEAD_DIM, seq), (0, 3, 1, 2)))
        new_kv[1].append(jnp.transpose(sbv_t.reshape(batch, SB_HEADS, HEAD_DIM, seq), (0, 3, 1, 2)))
        new_kv[2].append(dk_f.reshape(batch, seq, DIFF_HEADS, 2 * HEAD_DIM))
        new_kv[3].append(dv_f.reshape(batch, seq, DIFF_HEADS, DIFF_V_DIM))
        new_kv[4].append(p_s_flat[:, 1 * PIECE:2 * PIECE].reshape(nseq, n_new, SB_HEADS, HEAD_DIM))
        new_kv[5].append(p_s_flat[:, 2 * PIECE:3 * PIECE].reshape(nseq, n_new, SB_HEADS, HEAD_DIM))
        new_kv[6].append(p_s_flat[:, 4 * PIECE:5 * PIECE].reshape(nseq, n_new, DIFF_HEADS, 2 * HEAD_DIM))
        new_kv[7].append(p_s_flat[:, 5 * PIECE:6 * PIECE].reshape(nseq, n_new, DIFF_HEADS, DIFF_V_DIM))

    return (hp.reshape(batch, seq, d), hs.reshape(nseq, n_new, d)) + tuple(jnp.stack(v) for v in new_kv)
```

```python
import functools
import math

import jax
import jax.numpy as jnp
from jax import lax
from jax.experimental import pallas as pl
from jax.experimental.pallas import tpu as pltpu

F32 = jnp.float32
BF16 = jnp.bfloat16
I32 = jnp.int32
U32 = jnp.uint32

HEAD_DIM = 64
SB_HEADS = 8
DIFF_HEADS = 4
DIFF_V_DIM = 2 * HEAD_DIM
SB_WIDTH = SB_HEADS * HEAD_DIM
DIFF_QK_WIDTH = DIFF_HEADS * 2 * HEAD_DIM
DIFF_V_WIDTH = DIFF_HEADS * DIFF_V_DIM
PIECE = 512
N_PIECES = 6
N_EXPERTS = 32
TOP_K = 4
SWIGLU_LIMIT = 7.0
SWIGLU_ALPHA = 1.702
RMS_EPS = 1e-5
NEG_BIG = -1e30
QK_SCALE = HEAD_DIM ** -0.5
SB_SKIP_BELOW = -104.0
SB_REACH_MARGIN = 8.0
ALIBI_SLOPES = tuple(2.0 ** (-8.0 * (h + 1) / DIFF_HEADS) for h in range(DIFF_HEADS))

ATTN_BLOCK = 256
DIFF_BLOCK = 512
TOKEN_BLOCK = 512
PAGES_PER_STEP = 8
MOE_MAX_TOKENS = 3584
MOE_CHUNK = 512
MOE_PAD_ROWS = 8
VMEM_LIMIT = 56 * 1024 * 1024


def _cparams(sem, vmem=None):
    return pltpu.CompilerParams(dimension_semantics=sem, vmem_limit_bytes=vmem)


def _dot(a, b):
    return jnp.dot(a, b, preferred_element_type=F32)


def _dot_nt(a, b):
    return lax.dot_general(a, b, (((1,), (1,)), ((), ())), preferred_element_type=F32)


def _split_bf16(x):
    hi = x.astype(BF16)
    lo = (x - hi.astype(F32)).astype(BF16)
    return hi, lo


def _ada_kernel(c_ref, w_ref, b_ref, o_ref):
    c = c_ref[...]
    h = (c * jax.nn.sigmoid(c)).astype(BF16)
    o_ref[...] = _dot(h, w_ref[...].astype(BF16)) + b_ref[...]


def _ada(c, w, b):
    rows, d = c.shape
    n = w.shape[1]
    return pl.pallas_call(
        _ada_kernel,
        grid=(n // d,),
        in_specs=[pl.BlockSpec((rows, d), lambda j: (0, 0)),
                  pl.BlockSpec((d, d), lambda j: (0, j)),
                  pl.BlockSpec((1, d), lambda j: (0, j))],
        out_specs=pl.BlockSpec((rows, d), lambda j: (0, j)),
        out_shape=jax.ShapeDtypeStruct((rows, n), F32),
        compiler_params=_cparams(("arbitrary",)),
        name="ada",
    )(c, w, b.reshape(1, n))


def _norm_mod(x, nw, sh, sc):
    ms = jnp.mean(x * x, axis=-1, keepdims=True)
    y = x * lax.rsqrt(ms + RMS_EPS) * nw
    return y * (1.0 + sc) + sh


def _inproj_prompt_kernel(x_ref, sh_ref, sc_ref, nw_ref, w_ref, wt_ref,
                          sbq_o, sbk_o, sbv_o, dq_o, dk_o, dv_o,
                          sbk_t, sbv_t, dk_f, dv_f):
    u = _norm_mod(x_ref[...], nw_ref[...], sh_ref[0], sc_ref[0]).astype(BF16)
    tm = u.shape[0]
    hm_outs = (sbq_o, sbk_o, sbv_o, dq_o, dk_o, None)
    t_outs = (None, sbk_t, sbv_t, None, None, None)
    f_outs = (None, None, None, None, dk_f, dv_f)
    scales = (QK_SCALE, None, None, QK_SCALE, None, None)
    for piece in range(N_PIECES):
        p = _dot(u, w_ref[:, piece * PIECE:(piece + 1) * PIECE])
        if t_outs[piece] is not None:
            t_outs[piece][0] = _dot_nt(wt_ref[(piece - 1) * PIECE:piece * PIECE, :], u)
        if f_outs[piece] is not None:
            for h in range(DIFF_HEADS):
                f_outs[piece][pl.ds(h, tm, stride=DIFF_HEADS), :] = p[:, h * DIFF_V_DIM:(h + 1) * DIFF_V_DIM]
        if scales[piece] is not None:
            p = p * scales[piece]
        pb = p.astype(BF16)
        if hm_outs[piece] is not None:
            for h in range(PIECE // HEAD_DIM):
                hm_outs[piece][0, h] = pb[:, h * HEAD_DIM:(h + 1) * HEAD_DIM]
        else:
            for h in range(DIFF_HEADS):
                dv_o[0, h] = pb[:, h * DIFF_V_DIM:(h + 1) * DIFF_V_DIM]


def _inproj_prompt(x, sh, sc, nw, w_bf, wt_sbkv, batch, seq):
    n, d = x.shape
    tm = min(TOKEN_BLOCK, seq)
    nb = seq // tm
    hm = lambda i: (i // nb, 0, i % nb, 0)
    row = lambda i: (i, 0)
    mod = lambda i: (i // nb, 0, 0)
    hm64 = jax.ShapeDtypeStruct((batch, 8, seq, HEAD_DIM), BF16)
    hm128 = jax.ShapeDtypeStruct((batch, DIFF_HEADS, seq, DIFF_V_DIM), BF16)
    transposed = jax.ShapeDtypeStruct((batch, PIECE, seq), F32)
    head_rows = jax.ShapeDtypeStruct((n * DIFF_HEADS, DIFF_V_DIM), F32)
    return pl.pallas_call(
        _inproj_prompt_kernel,
        grid=(n // tm,),
        in_specs=[pl.BlockSpec((tm, d), row),
                  pl.BlockSpec((1, 1, d), mod),
                  pl.BlockSpec((1, 1, d), mod),
                  pl.BlockSpec((1, d), lambda i: (0, 0)),
                  pl.BlockSpec((d, N_PIECES * PIECE), lambda i: (0, 0)),
                  pl.BlockSpec((2 * PIECE, d), lambda i: (0, 0))],
        out_specs=[pl.BlockSpec((1, 8, tm, HEAD_DIM), hm)] * 5
                  + [pl.BlockSpec((1, DIFF_HEADS, tm, DIFF_V_DIM), hm)]
                  + [pl.BlockSpec((1, PIECE, tm), lambda i: (i // nb, 0, i % nb))] * 2
                  + [pl.BlockSpec((tm * DIFF_HEADS, DIFF_V_DIM), row)] * 2,
        out_shape=[hm64] * 5 + [hm128] + [transposed] * 2 + [head_rows] * 2,
        compiler_params=_cparams(("arbitrary",), VMEM_LIMIT),
        name="inproj_prompt",
    )(x, sh, sc, nw, w_bf, wt_sbkv)


def _inproj_sample_kernel(x_ref, sh_ref, sc_ref, nw_ref, w_ref, o_ref):
    u = _norm_mod(x_ref[...], nw_ref[...], sh_ref[...], sc_ref[...]).astype(BF16)
    for piece in range(N_PIECES):
        sl = slice(piece * PIECE, (piece + 1) * PIECE)
        o_ref[:, sl] = _dot(u, w_ref[:, sl])


def _inproj_sample(x, sh, sc, nw, w_bf):
    n, d = x.shape
    tm = min(TOKEN_BLOCK, n)
    row = lambda i: (i, 0)
    return pl.pallas_call(
        _inproj_sample_kernel,
        grid=(n // tm,),
        in_specs=[pl.BlockSpec((tm, d), row), pl.BlockSpec((tm, d), row), pl.BlockSpec((tm, d), row),
                  pl.BlockSpec((1, d), lambda i: (0, 0)),
                  pl.BlockSpec((d, N_PIECES * PIECE), lambda i: (0, 0))],
        out_specs=pl.BlockSpec((tm, N_PIECES * PIECE), row),
        out_shape=jax.ShapeDtypeStruct((n, N_PIECES * PIECE), F32),
        compiler_params=_cparams(("arbitrary",), VMEM_LIMIT),
        name="inproj_sample",
    )(x, sh, sc, nw, w_bf)


def _sb_logs(z):
    t = jnp.log1p(jnp.exp(-jnp.abs(z)))
    return jnp.minimum(z, 0.0) - t, -jnp.maximum(z, 0.0) - t


def _strict_upper(t):
    r = lax.broadcasted_iota(I32, (t, t), 0)
    c = lax.broadcasted_iota(I32, (t, t), 1)
    return jnp.where(r > c, 1.0, 0.0).astype(BF16)


def _suffix_sum(ls, tri):
    hi, lo = _split_bf16(ls)
    return _dot(hi, tri) + _dot(lo, tri)


def _sb_prompt_kernel(q_ref, k_ref, v_ref, o_ref, carry_ref, acc_ref, *, blk):
    i = pl.program_id(1)
    tri = _strict_upper(blk)
    r = lax.broadcasted_iota(I32, (blk, blk), 0)
    c = lax.broadcasted_iota(I32, (blk, blk), 1)
    vis = c < r

    def block(j, diag):
        start = pl.multiple_of(j * blk, blk)
        for h in range(SB_HEADS):
            k = k_ref[0, h, pl.ds(start, blk), :]
            v = v_ref[0, h, pl.ds(start, blk), :]
            lb, ls = _sb_logs(_dot_nt(q_ref[0, h], k))
            if diag:
                ls = jnp.where(vis, ls, 0.0)
            carry = carry_ref[h]
            a = jnp.exp(lb + _suffix_sum(ls, tri) + carry)
            if diag:
                a = jnp.where(vis, a, 0.0)
            acc_ref[h] = acc_ref[h] + _dot(a.astype(BF16), v)
            carry_ref[h] = carry + jnp.sum(ls, axis=1, keepdims=True)

    carry_ref[...] = jnp.zeros_like(carry_ref)
    acc_ref[...] = jnp.zeros_like(acc_ref)
    block(i, True)

    def cond(s):
        return jnp.logical_and(s[0] >= 0, s[1] > SB_SKIP_BELOW)

    def body(s):
        block(s[0], False)
        return s[0] - 1, jnp.max(carry_ref[...])

    lax.while_loop(cond, body, (i - 1, jnp.max(carry_ref[...])))
    o_ref[0] = jnp.concatenate([acc_ref[h] for h in range(SB_HEADS)], axis=1).astype(o_ref.dtype)


def _sb_prompt(q, k, v):
    b, h, s, dh = q.shape
    blk = min(ATTN_BLOCK, s)
    return pl.pallas_call(
        functools.partial(_sb_prompt_kernel, blk=blk),
        grid=(b, s // blk),
        in_specs=[pl.BlockSpec((1, h, blk, dh), lambda bi, i: (bi, 0, i, 0)),
                  pl.BlockSpec((1, h, s, dh), lambda bi, i: (bi, 0, 0, 0)),
                  pl.BlockSpec((1, h, s, dh), lambda bi, i: (bi, 0, 0, 0))],
        out_specs=pl.BlockSpec((1, blk, h * dh), lambda bi, i: (bi, i, 0)),
        out_shape=jax.ShapeDtypeStruct((b, s, h * dh), BF16),
        scratch_shapes=[pltpu.VMEM((h, blk, 1), F32), pltpu.VMEM((h, blk, dh), F32)],
        compiler_params=_cparams(("arbitrary", "arbitrary"), VMEM_LIMIT),
        name="sb_prompt",
    )(q, k, v)


def _lambda_from(lamv_ref, lambda_init):
    l1 = jnp.sum(lamv_ref[0:1, :] * lamv_ref[1:2, :], axis=1, keepdims=True)
    l2 = jnp.sum(lamv_ref[2:3, :] * lamv_ref[3:4, :], axis=1, keepdims=True)
    return jnp.exp(l1) - jnp.exp(l2) + lambda_init


def _head_slope(h):
    out = jnp.float32(ALIBI_SLOPES[-1])
    for hh in range(DIFF_HEADS - 2, -1, -1):
        out = jnp.where(h == hh, jnp.float32(ALIBI_SLOPES[hh]), out)
    return out


def _diff_prompt_kernel(lamv_ref, sw_ref, q_ref, k_ref, v_ref, o_ref, *, blk, lambda_init):
    h = pl.program_id(1)
    i = pl.program_id(2)
    slope = _head_slope(h)
    r = lax.broadcasted_iota(I32, (blk, blk), 0)
    c = lax.broadcasted_iota(I32, (blk, blk), 1)
    srel = slope * (r - c).astype(F32)
    causal = c <= r
    qs = (q_ref[0, 0], q_ref[0, 1])

    def block(j, st, diag):
        start = pl.multiple_of(j * blk, blk)
        v = v_ref[0, 0, pl.ds(start, blk), :]
        off = slope * ((i - j) * blk).astype(F32)
        new = []
        for comp in range(2):
            m, l, acc = st[3 * comp:3 * comp + 3]
            k = k_ref[0, comp, pl.ds(start, blk), :]
            s = _dot_nt(qs[comp], k) - srel
            if diag:
                s = jnp.where(causal, s, NEG_BIG)
            m_new = jnp.maximum(m, jnp.max(s, axis=1, keepdims=True) - off)
            p = jnp.exp(s - (m_new + off))
            alpha = jnp.exp(m - m_new)
            l = l * alpha + jnp.sum(p, axis=1, keepdims=True)
            acc = acc * alpha + _dot(p.astype(BF16), v)
            new += [m_new, l, acc]
        return tuple(new)

    init = (jnp.full((blk, 1), NEG_BIG, F32), jnp.zeros((blk, 1), F32),
            jnp.zeros((blk, DIFF_V_DIM), F32)) * 2
    st = lax.fori_loop(0, i, lambda j, st: block(j, st, False), init)
    st = block(i, st, True)
    lam = _lambda_from(lamv_ref, lambda_init)
    o = st[2] / st[1] - lam * (st[5] / st[4])
    o = o * lax.rsqrt(jnp.mean(o * o, axis=1, keepdims=True) + RMS_EPS)
    o_ref[0] = (o * sw_ref[...] * (1.0 - lambda_init)).astype(o_ref.dtype)


def _diff_prompt(lamv, sw, q, k, v, lambda_init):
    b, h2, s, dh = q.shape
    blk = min(DIFF_BLOCK, s)
    return pl.pallas_call(
        functools.partial(_diff_prompt_kernel, blk=blk, lambda_init=lambda_init),
        grid=(b, DIFF_HEADS, s // blk),
        in_specs=[pl.BlockSpec((4, HEAD_DIM), lambda bi, hi, i: (0, 0)),
                  pl.BlockSpec((1, DIFF_V_DIM), lambda bi, hi, i: (0, 0)),
                  pl.BlockSpec((1, 2, blk, dh), lambda bi, hi, i: (bi, hi, i, 0)),
                  pl.BlockSpec((1, 2, s, dh), lambda bi, hi, i: (bi, hi, 0, 0)),
                  pl.BlockSpec((1, 1, s, DIFF_V_DIM), lambda bi, hi, i: (bi, hi, 0, 0))],
        out_specs=pl.BlockSpec((1, blk, DIFF_V_DIM), lambda bi, hi, i: (bi, i, hi)),
        out_shape=jax.ShapeDtypeStruct((b, s, DIFF_V_WIDTH), BF16),
        compiler_params=_cparams(("arbitrary", "arbitrary", "arbitrary"), VMEM_LIMIT),
        name="diff_prompt",
    )(lamv, sw, q, k, v)


def _sample_kernel(pt_ref, sb_steps_ref, lamv_ref, sw_ref, qsb_ref, qd_ref, new_ref, *rest,
                   n_new, page, n_pages, pages_per_step, lambda_init):
    g_ = pages_per_step
    sbk_pages = rest[0 * g_:1 * g_]
    sbv_pages = rest[1 * g_:2 * g_]
    dk_pages = rest[2 * g_:3 * g_]
    dv_pages = rest[3 * g_:4 * g_]
    o_ref = rest[4 * g_]
    sb_carry, sb_acc, d_m, d_l, d_acc = rest[4 * g_ + 1:]
    t = pl.program_id(1)
    rows = 8 * n_new
    hrows = 2 * n_new
    past_len = n_pages * page

    ridx = lax.broadcasted_iota(I32, (rows, 1), 0)
    qidx = ridx % n_new
    grp = ridx // n_new
    slope_row = jnp.zeros((rows, 1), F32)
    for hh in range(DIFF_HEADS):
        slope_row = jnp.where(grp // 2 == hh, jnp.float32(ALIBI_SLOPES[hh]), slope_row)
    qpos = (past_len + qidx).astype(F32)

    @pl.when(t == 0)
    def _():
        qsb = qsb_ref[0].astype(F32)
        col = lax.broadcasted_iota(I32, (rows, DIFF_QK_WIDTH), 1)
        qd = jnp.where(col // DIFF_V_DIM == grp // 2,
                       jnp.concatenate([qd_ref[0].astype(F32)] * DIFF_HEADS, axis=1), 0.0)
        carry = jnp.zeros((rows, 1), F32)
        acc = jnp.zeros((rows, SB_WIDTH), F32)
        m = jnp.full((rows, 1), NEG_BIG, F32)
        l = jnp.zeros((rows, 1), F32)
        dacc = jnp.zeros((rows, DIFF_V_DIM), F32)
        for j in range(n_new - 1, -1, -1):
            sbk = new_ref[0, j:j + 1, 1 * PIECE:2 * PIECE]
            sbv = new_ref[0, j:j + 1, 2 * PIECE:3 * PIECE]
            dk = new_ref[0, j:j + 1, 4 * PIECE:5 * PIECE]
            dv = new_ref[0, j:j + 1, 5 * PIECE:6 * PIECE]
            z = jnp.sum(qsb * sbk, axis=1, keepdims=True)
            lb, ls = _sb_logs(z)
            vis = j < qidx
            ls = jnp.where(vis, ls, 0.0)
            a = jnp.where(vis, jnp.exp(lb + carry), 0.0)
            acc = acc + a * sbv
            carry = carry + ls
            s = jnp.sum(qd * dk, axis=1, keepdims=True) - slope_row * (qidx - j).astype(F32)
            s = jnp.where(j <= qidx, s, NEG_BIG)
            m_new = jnp.maximum(m, s)
            p = jnp.exp(s - m_new)
            alpha = jnp.exp(m - m_new)
            l = l * alpha + p
            dv_rows = jnp.concatenate(
                [jnp.broadcast_to(dv[:, hh * DIFF_V_DIM:(hh + 1) * DIFF_V_DIM], (hrows, DIFF_V_DIM))
                 for hh in range(DIFF_HEADS)], axis=0)
            dacc = dacc * alpha + p * dv_rows
            m = m_new
        sb_carry[...] = carry
        sb_acc[...] = acc
        d_m[...] = m
        d_l[...] = l
        d_acc[...] = dacc

    @pl.when(jnp.logical_and(t < sb_steps_ref[pl.program_id(0)], jnp.max(sb_carry[...]) > SB_SKIP_BELOW))
    def _():
        qsb = qsb_ref[0]
        tri = _strict_upper(page)
        z = jnp.concatenate([_dot(qsb, sbk_pages[g][0].astype(BF16)) for g in range(g_)], axis=0)
        lb, ls = _sb_logs(z)
        suffix = _suffix_sum(ls, tri)
        rowsum = jnp.sum(ls, axis=1, keepdims=True)
        carry = sb_carry[...]
        carries = []
        for g in range(g_):
            carries.append(carry)
            carry = carry + rowsum[g * rows:(g + 1) * rows]
        a = jnp.exp(lb + suffix + jnp.concatenate(carries, axis=0)).astype(BF16)
        acc = sb_acc[...]
        for g in range(g_):
            acc = acc + _dot_nt(a[g * rows:(g + 1) * rows], sbv_pages[g][0].astype(BF16))
        sb_acc[...] = acc
        sb_carry[...] = carry

    qd = qd_ref[0]
    lane = lax.broadcasted_iota(I32, (1, page), 1).astype(F32)
    ss, bases = [], []
    for g in range(g_):
        pg = n_pages - 1 - (t * g_ + g)
        for hh in range(DIFF_HEADS):
            kh = dk_pages[g][0, pl.ds(hh, page, stride=DIFF_HEADS), :].astype(BF16)
            ss.append(_dot_nt(qd[hh * hrows:(hh + 1) * hrows], kh))
        bases.append(qpos - (pg * page).astype(F32))
    s = jnp.concatenate(ss, axis=0)
    base = jnp.concatenate(bases, axis=0)
    slope_all = jnp.concatenate([slope_row] * g_, axis=0)
    s = s - slope_all * (base - lane)
    rowmax = jnp.max(s, axis=1, keepdims=True)
    m_old = d_m[...]
    m_new = m_old
    for g in range(g_):
        m_new = jnp.maximum(m_new, rowmax[g * rows:(g + 1) * rows])
    p = jnp.exp(s - jnp.concatenate([m_new] * g_, axis=0))
    rowsum = jnp.sum(p, axis=1, keepdims=True)
    alpha = jnp.exp(m_old - m_new)
    l = d_l[...] * alpha
    for g in range(g_):
        l = l + rowsum[g * rows:(g + 1) * rows]
    pb = p.astype(BF16)
    pvs = []
    for hh in range(DIFF_HEADS):
        pv = jnp.zeros((hrows, DIFF_V_DIM), F32)
        for g in range(g_):
            vh = dv_pages[g][0, pl.ds(hh, page, stride=DIFF_HEADS), :].astype(BF16)
            pv = pv + _dot(pb[g * rows + hh * hrows:g * rows + (hh + 1) * hrows], vh)
        pvs.append(pv)
    d_acc[...] = d_acc[...] * alpha + jnp.concatenate(pvs, axis=0)
    d_l[...] = l
    d_m[...] = m_new

    @pl.when(t == pl.num_programs(1) - 1)
    def _():
        col = lax.broadcasted_iota(I32, (rows, SB_WIDTH), 1)
        qrow = lax.broadcasted_iota(I32, (8, rows), 0)
        rcol = lax.broadcasted_iota(I32, (8, rows), 1)
        sel = jnp.where(rcol % n_new == qrow, 1.0, 0.0).astype(BF16)
        hi, lo = _split_bf16(jnp.where(col // HEAD_DIM == grp, sb_acc[...], 0.0))
        o_sb = _dot(sel, hi) + _dot(sel, lo)

        lam = _lambda_from(lamv_ref, lambda_init)
        on = d_acc[...] / d_l[...]
        segs = []
        for hh in range(DIFF_HEADS):
            both = on[hh * hrows:(hh + 1) * hrows]
            seg = both - lam * pltpu.roll(both, n_new, axis=0)
            seg = seg * lax.rsqrt(jnp.mean(seg * seg, axis=1, keepdims=True) + RMS_EPS)
            segs.append(seg * sw_ref[...] * (1.0 - lambda_init))
        full = jnp.concatenate([o_sb] + segs, axis=1)
        o_ref[0] = full[:n_new].astype(o_ref.dtype)


def _sb_reach_kernel(pt_ref, qsb_ref, new_ref, *rest, n_new, pages_per_step):
    k_pages = rest[:pages_per_step]
    o_ref = rest[pages_per_step]
    rows = 8 * n_new
    qidx = lax.broadcasted_iota(I32, (rows, 1), 0) % n_new
    qsb = qsb_ref[0]
    qsb_f = qsb.astype(F32)
    carry = jnp.zeros((rows, 1), F32)
    for j in range(n_new):
        z = jnp.sum(qsb_f * new_ref[0, j:j + 1, 1 * PIECE:2 * PIECE], axis=1, keepdims=True)
        carry = carry + jnp.where(j < qidx, _sb_logs(z)[1], 0.0)
    for g in range(pages_per_step):
        ls = _sb_logs(_dot(qsb, k_pages[g][0].astype(BF16)))[1]
        carry = carry + jnp.sum(ls, axis=1, keepdims=True)
    settled = jnp.max(carry, axis=0, keepdims=True) < SB_SKIP_BELOW - SB_REACH_MARGIN
    o_ref[0] = jnp.broadcast_to(jnp.where(settled, 1, 0).astype(I32), o_ref.shape[1:])


def _sb_reach(page_table, qsb_bd, p_new, cache_k, g_):
    nseq, n_pages = page_table.shape
    n_new = p_new.shape[1]
    rows = 8 * n_new
    rows_per_page, page = cache_k.shape[1], cache_k.shape[2]
    specs = [pl.BlockSpec((1, rows_per_page, page), lambda b, pt, g=g: (pt[b, n_pages - 1 - g], 0, 0))
             for g in range(g_)]
    grid_spec = pltpu.PrefetchScalarGridSpec(
        num_scalar_prefetch=1,
        grid=(nseq,),
        in_specs=[pl.BlockSpec((1, rows, SB_WIDTH), lambda b, pt: (b, 0, 0)),
                  pl.BlockSpec((1, n_new, N_PIECES * PIECE), lambda b, pt: (b, 0, 0))] + specs,
        out_specs=pl.BlockSpec((1, 8, 128), lambda b, pt: (b, 0, 0)),
    )
    settled = pl.pallas_call(
        functools.partial(_sb_reach_kernel, n_new=n_new, pages_per_step=g_),
        grid_spec=grid_spec,
        out_shape=jax.ShapeDtypeStruct((nseq, 8, 128), I32),
        compiler_params=_cparams(("arbitrary",), VMEM_LIMIT),
        name="sb_reach",
    )(page_table, qsb_bd, p_new, *([cache_k] * g_))
    return jnp.where(settled[:, 0, 0] == 1, 1, n_pages // g_).astype(I32)


def _sample_attention(page_table, lamv, sw, qsb_bd, qd_bd, p_new, caches, lambda_init):
    nseq, n_pages = page_table.shape
    n_new = p_new.shape[1]
    assert 2 * n_new == 8, "differential rows per head must fill one sublane group"
    rows_per_page = caches[0].shape[1]
    page = caches[0].shape[2]
    g_ = math.gcd(PAGES_PER_STEP, n_pages)
    rows = 8 * n_new
    sb_steps = _sb_reach(page_table, qsb_bd, p_new, caches[0], g_)

    def page_spec(g, sb):
        if sb:
            return pl.BlockSpec(
                (1, rows_per_page, page),
                lambda b, t, pt, ns, g=g: (pt[b, n_pages - 1 - (jnp.minimum(t, ns[b] - 1) * g_ + g)], 0, 0))
        return pl.BlockSpec((1, rows_per_page, page),
                            lambda b, t, pt, ns, g=g: (pt[b, n_pages - 1 - (t * g_ + g)], 0, 0))

    cache_specs, cache_args = [], []
    for ci, cache in enumerate(caches):
        for g in range(g_):
            cache_specs.append(page_spec(g, ci < 2))
            cache_args.append(cache)
    grid_spec = pltpu.PrefetchScalarGridSpec(
        num_scalar_prefetch=2,
        grid=(nseq, n_pages // g_),
        in_specs=[pl.BlockSpec((4, HEAD_DIM), lambda b, t, pt, ns: (0, 0)),
                  pl.BlockSpec((1, DIFF_V_DIM), lambda b, t, pt, ns: (0, 0)),
                  pl.BlockSpec((1, rows, SB_WIDTH), lambda b, t, pt, ns: (b, 0, 0)),
                  pl.BlockSpec((1, rows, DIFF_V_DIM), lambda b, t, pt, ns: (b, 0, 0)),
                  pl.BlockSpec((1, n_new, N_PIECES * PIECE), lambda b, t, pt, ns: (b, 0, 0))] + cache_specs,
        out_specs=pl.BlockSpec((1, n_new, 2 * PIECE), lambda b, t, pt, ns: (b, 0, 0)),
        scratch_shapes=[pltpu.VMEM((rows, 1), F32), pltpu.VMEM((rows, SB_WIDTH), F32),
                        pltpu.VMEM((rows, 1), F32), pltpu.VMEM((rows, 1), F32),
                        pltpu.VMEM((rows, DIFF_V_DIM), F32)],
    )
    return pl.pallas_call(
        functools.partial(_sample_kernel, n_new=n_new, page=page, n_pages=n_pages,
                          pages_per_step=g_, lambda_init=lambda_init),
        grid_spec=grid_spec,
        out_shape=jax.ShapeDtypeStruct((nseq, n_new, 2 * PIECE), BF16),
        compiler_params=_cparams(("arbitrary", "arbitrary"), VMEM_LIMIT),
        name="sample_attention",
    )(page_table, sb_steps, lamv, sw, qsb_bd, qd_bd, p_new, *cache_args)


def _sb_query_rows(q):
    nseq, n_new, w = q.shape
    tiled = jnp.broadcast_to(q[:, None], (nseq, 8, n_new, w)).reshape(nseq, 8 * n_new, w)
    grp = (jnp.arange(8 * n_new) // n_new)[:, None]
    col = (jnp.arange(w) // HEAD_DIM)[None, :]
    return jnp.where(grp == col, tiled * QK_SCALE, 0.0).astype(BF16)


def _diff_query_rows(q):
    nseq, n_new, _ = q.shape
    q5 = q.reshape(nseq, n_new, DIFF_HEADS, 2, HEAD_DIM) * QK_SCALE
    q5 = jnp.transpose(q5, (0, 2, 3, 1, 4))
    eye = jnp.eye(2, dtype=q.dtype)[None, None, :, None, :, None]
    out = q5[:, :, :, :, None, :] * eye
    return out.reshape(nseq, 8 * n_new, DIFF_V_DIM).astype(BF16)


def _post_kernel(x_ref, osb_ref, od_ref, g_ref, sh_ref, sc_ref, nw_ref, wo_ref, rw_hi_ref, rw_lo_ref,
                 rb_ref, h_ref, f_ref, gates_ref, *, per_token):
    take = (lambda r: r[...]) if per_token else (lambda r: r[0])
    proj = _dot(osb_ref[...], wo_ref[:SB_WIDTH, :]) + _dot(od_ref[...], wo_ref[SB_WIDTH:, :])
    h = x_ref[...] + take(g_ref) * proj
    h_ref[...] = h
    f = _norm_mod(h, nw_ref[...], take(sh_ref), take(sc_ref))

    f_ref[...] = f

    f_hi, f_lo = _split_bf16(f)
    logits = (_dot(f_hi, rw_hi_ref[...]) + _dot(f_lo, rw_hi_ref[...]) + _dot(f_hi, rw_lo_ref[...])
              + rb_ref[...])
    lane = lax.broadcasted_iota(I32, logits.shape, 1).astype(F32)
    work = logits
    firsts, vals = [], []
    for _ in range(TOP_K):
        mx = jnp.max(work, axis=1, keepdims=True)
        first = jnp.min(jnp.where(work == mx, lane, float(N_EXPERTS)), axis=1, keepdims=True)
        firsts.append(first)
        vals.append(mx)
        work = jnp.where(lane == first, -jnp.inf, work)
    es = [jnp.exp(v - vals[0]) for v in vals]
    denom = es[0] + es[1] + es[2] + es[3]
    slot = lax.broadcasted_iota(I32, gates_ref.shape, 1)
    route = jnp.zeros(gates_ref.shape, F32)
    for k in range(TOP_K):
        route = jnp.where(slot == k, firsts[k], route)
        route = jnp.where(slot == TOP_K + k, es[k] / denom, route)
    gates_ref[...] = route


def _post(x, osb, od, g, sh, sc, nw, wo_bf, rw_hi, rw_lo, rb, tokens_per_group):
    n, d = x.shape
    per_token = tokens_per_group is None
    tm = min(TOKEN_BLOCK, n if per_token else tokens_per_group)
    row = lambda i: (i, 0)
    const = lambda i: (0, 0)
    if per_token:
        mod_spec = pl.BlockSpec((tm, d), row)
    else:
        nb = tokens_per_group // tm
        mod_spec = pl.BlockSpec((1, 1, d), lambda i: (i // nb, 0, 0))
    return pl.pallas_call(
        functools.partial(_post_kernel, per_token=per_token),
        grid=(n // tm,),
        in_specs=[pl.BlockSpec((tm, d), row),
                  pl.BlockSpec((tm, SB_WIDTH), row),
                  pl.BlockSpec((tm, DIFF_V_WIDTH), row),
                  mod_spec, mod_spec, mod_spec,
                  pl.BlockSpec((1, d), const),
                  pl.BlockSpec((SB_WIDTH + DIFF_V_WIDTH, d), const),
                  pl.BlockSpec((d, N_EXPERTS), const),
                  pl.BlockSpec((d, N_EXPERTS), const),
                  pl.BlockSpec((1, N_EXPERTS), const)],
        out_specs=[pl.BlockSpec((tm, d), row),
                   pl.BlockSpec((tm, d), row),
                   pl.BlockSpec((tm, 2 * TOP_K), row)],
        out_shape=[jax.ShapeDtypeStruct((n, d), F32),
                   jax.ShapeDtypeStruct((n, d), F32),
                   jax.ShapeDtypeStruct((n, 2 * TOP_K), F32)],
        compiler_params=_cparams(("arbitrary",), VMEM_LIMIT),
        name="post_attention",
    )(x, osb, od, g, sh, sc, nw, wo_bf, rw_hi, rw_lo, rb)


def _moe_plan(n_tokens):
    nblk = 1
    while n_tokens % nblk or n_tokens // nblk > MOE_MAX_TOKENS or (n_tokens // nblk) % 8:
        nblk += 1
    ttok = n_tokens // nblk
    mc = min(MOE_CHUNK, ttok)
    smax = N_EXPERTS + (ttok * TOP_K) // mc
    return nblk, ttok, mc, smax


def _routing(gates, nblk, ttok, mc, smax):
    n = gates.shape[0]
    per_blk = ttok * TOP_K
    tpad = 1 << (ttok - 1).bit_length()
    topi = gates[:, :TOP_K].astype(I32)
    keys = (topi * tpad + (jnp.arange(n, dtype=I32) % ttok)[:, None]).reshape(nblk, per_blk)
    skeys, sw = lax.sort((keys, gates[:, TOP_K:].reshape(nblk, per_blk)), dimension=1, num_keys=1)
    bounds = jnp.arange(N_EXPERTS + 1, dtype=I32) * tpad
    first = jnp.sum((skeys[:, :, None] < bounds[None, None, :]).astype(I32), axis=1)
    cnt = first[:, 1:] - first[:, :-1]
    nch = (cnt + mc - 1) // mc
    ends = jnp.cumsum(nch, axis=1)
    starts = ends - nch
    total = ends[:, -1]
    s_ar = jnp.arange(smax, dtype=I32)
    slot_e = jnp.sum((s_ar[None, :, None] >= ends[:, None, :]).astype(I32), axis=-1)
    valid = s_ar[None, :] < total[:, None]
    last_e = jnp.take_along_axis(slot_e, jnp.maximum(total - 1, 0)[:, None], axis=1)
    slot_e = jnp.where(valid, jnp.minimum(slot_e, N_EXPERTS - 1), jnp.minimum(last_e, N_EXPERTS - 1))
    chunk = s_ar[None, :] - jnp.take_along_axis(starts, slot_e, axis=1)
    slot_cnt = jnp.where(valid, jnp.clip(jnp.take_along_axis(cnt, slot_e, axis=1) - chunk * mc, 0, mc), 0)
    lane = jnp.arange(mc, dtype=I32)
    src = jnp.take_along_axis(first[:, :-1], slot_e, axis=1)[:, :, None] + chunk[:, :, None] * mc + lane[None, None, :]
    src = (jnp.clip(src, 0, per_blk - 1) + jnp.arange(nblk, dtype=I32)[:, None, None] * per_blk).reshape(-1)
    live = lane[None, None, :] < slot_cnt[:, :, None]
    grow = (0, nblk * ttok * N_EXPERTS - nblk * per_blk)
    tok = (jnp.take(jnp.pad(skeys.reshape(-1), grow), src) & (tpad - 1)).reshape(nblk, smax, mc)
    gw = jnp.where(live, jnp.take(jnp.pad(sw.reshape(-1), grow), src).reshape(nblk, smax, mc), 0.0)
    tok = jnp.where(live, tok, ttok + lane[None, None, :] % MOE_PAD_ROWS)
    return slot_e.reshape(-1), slot_cnt.reshape(-1).astype(I32), tok, gw


def _moe_kernel(se_ref, sc_ref, x_hbm, idx_ref, gw_ref, wgu_ref, bgu_ref, wd_ref, bd_ref, out_hbm,
                x_ref, acc_ref, xg_ref, y_ref, sem, *, ttok, mc, smax, d_ff):
    tb = pl.program_id(0)
    s = pl.program_id(1)
    cnt = sc_ref[tb * smax + s]

    @pl.when(jnp.logical_and(tb == 0, s == 0))
    def _():
        xg_ref[...] = jnp.zeros_like(xg_ref)

    @pl.when(s == 0)
    def _():
        cp = pltpu.make_async_copy(x_hbm.at[pl.ds(tb * ttok, ttok), :], x_ref.at[pl.ds(0, ttok), :], sem.at[0])
        cp.start()
        x_ref[pl.ds(ttok, MOE_PAD_ROWS), :] = jnp.zeros((MOE_PAD_ROWS, x_ref.shape[1]), F32)
        acc_ref[...] = jnp.zeros_like(acc_ref)
        cp.wait()

    @pl.when(cnt > 0)
    def _():
        ngrp = (cnt + MOE_PAD_ROWS - 1) // MOE_PAD_ROWS

        def gather(gi, carry):
            for u in range(MOE_PAD_ROWS):
                r = gi * MOE_PAD_ROWS + u
                xg_ref[pl.ds(r, 1), :] = x_ref[pl.ds(idx_ref[0, 0, r], 1), :]
            return carry

        lax.fori_loop(0, ngrp, gather, 0)
        gu = _dot(xg_ref[...].astype(BF16), wgu_ref[0]) + bgu_ref[0]
        g = jnp.minimum(gu[:, :d_ff], SWIGLU_LIMIT)
        u = jnp.clip(gu[:, d_ff:], -SWIGLU_LIMIT, SWIGLU_LIMIT)
        act = ((u + 1.0) * g * jax.nn.sigmoid(SWIGLU_ALPHA * g)).astype(BF16)
        y_ref[...] = _dot(act, wd_ref[0]) + bd_ref[0]

        def scatter(gi, carry):
            for half in range(2):
                rs = [gi * MOE_PAD_ROWS + half * 4 + u_ for u_ in range(4)]
                toks = [idx_ref[0, 0, r] for r in rs]
                new = [acc_ref[pl.ds(t_, 1), :] + gw_ref[0, 0, r] * y_ref[pl.ds(r, 1), :]
                       for t_, r in zip(toks, rs)]
                for t_, v_ in zip(toks, new):
                    acc_ref[pl.ds(t_, 1), :] = v_
            return carry

        lax.fori_loop(0, ngrp, scatter, 0)

    @pl.when(s == smax - 1)
    def _():
        cp = pltpu.make_async_copy(acc_ref.at[pl.ds(0, ttok), :], out_hbm.at[pl.ds(tb * ttok, ttok), :],
                                   sem.at[1])
        cp.start()
        cp.wait()


def _moe(x, gates, wgu_bf, bgu, wd_bf, bd):
    n, d = x.shape
    d_ff = wd_bf.shape[1]
    nblk, ttok, mc, smax = _moe_plan(n)
    slot_e, slot_cnt, tok, gw = _routing(gates, nblk, ttok, mc, smax)
    slot = lambda tb, s, se, sc: (tb * smax + s, 0, 0)
    expert = lambda tb, s, se, sc: (se[tb * smax + s], 0, 0)
    grid_spec = pltpu.PrefetchScalarGridSpec(
        num_scalar_prefetch=2,
        grid=(nblk, smax),
        in_specs=[pl.BlockSpec(memory_space=pl.ANY),
                  pl.BlockSpec((1, 1, mc), slot, memory_space=pltpu.SMEM),
                  pl.BlockSpec((1, 1, mc), slot, memory_space=pltpu.SMEM),
                  pl.BlockSpec((1, d, 2 * d_ff), expert),
                  pl.BlockSpec((1, 1, 2 * d_ff), expert),
                  pl.BlockSpec((1, d_ff, d), expert),
                  pl.BlockSpec((1, 1, d), expert)],
        out_specs=pl.BlockSpec(memory_space=pl.ANY),
        scratch_shapes=[pltpu.VMEM((ttok + MOE_PAD_ROWS, d), F32), pltpu.VMEM((ttok + MOE_PAD_ROWS, d), F32),
                        pltpu.VMEM((mc, d), F32),
                        pltpu.VMEM((mc, d), F32), pltpu.SemaphoreType.DMA((2,))],
    )
    return pl.pallas_call(
        functools.partial(_moe_kernel, ttok=ttok, mc=mc, smax=smax, d_ff=d_ff),
        grid_spec=grid_spec,
        out_shape=jax.ShapeDtypeStruct((n, d), F32),
        compiler_params=_cparams(("arbitrary", "arbitrary"), VMEM_LIMIT),
        name="moe",
    )(slot_e, slot_cnt, x, tok.reshape(nblk * smax, 1, mc), gw.reshape(nblk * smax, 1, mc),
      wgu_bf, bgu, wd_bf, bd)


def _final_kernel(h_ref, f_ref, g_ref, nw_ref, o_ref, *, per_token, last):
    g = g_ref[...] if per_token else g_ref[0]
    h = h_ref[...] + g * f_ref[...]
    if last:
        h = h * lax.rsqrt(jnp.mean(h * h, axis=-1, keepdims=True) + RMS_EPS) * nw_ref[...]
    o_ref[...] = h


def _final(h, f_all, row0, g, nw, tokens_per_group, last):
    n, d = h.shape
    per_token = tokens_per_group is None
    tm = min(TOKEN_BLOCK, n if per_token else tokens_per_group)
    assert row0 % tm == 0
    b0 = row0 // tm
    row = lambda i: (i, 0)
    if per_token:
        g_spec = pl.BlockSpec((tm, d), row)
    else:
        nb = tokens_per_group // tm
        g_spec = pl.BlockSpec((1, 1, d), lambda i: (i // nb, 0, 0))
    return pl.pallas_call(
        functools.partial(_final_kernel, per_token=per_token, last=last),
        grid=(n // tm,),
        in_specs=[pl.BlockSpec((tm, d), row),
                  pl.BlockSpec((tm, d), lambda i: (b0 + i, 0)),
                  g_spec,
                  pl.BlockSpec((1, d), lambda i: (0, 0))],
        out_specs=pl.BlockSpec((tm, d), row),
        out_shape=jax.ShapeDtypeStruct((n, d), F32),
        compiler_params=_cparams(("arbitrary",), VMEM_LIMIT),
        name="final",
    )(h, f_all, g, nw)


def kernel(x_prompt, x_sample, cache_sb_k, cache_sb_v, cache_diff_k, cache_diff_v, page_table, c_prompt, c_sample, ada_w, ada_b, norm_attn_w, w_in, diff_lambda_q1, diff_lambda_k1, diff_lambda_q2, diff_lambda_k2, diff_subln_w, w_out, norm_ffn_w, router_w, router_b, w_gate_up, b_gate_up, w_down, b_down, final_norm_w):
    batch, seq, d = x_prompt.shape
    nseq, n_new, _ = x_sample.shape
    depth = ada_w.shape[0]
    n_p, n_s = batch * seq, nseq * n_new
    n_pool, page = cache_sb_k.shape[1], cache_sb_k.shape[2]

    hp = x_prompt.reshape(n_p, d)
    hs = x_sample.reshape(n_s, d)
    c_all = jnp.concatenate([c_prompt, c_sample], axis=0)
    new_kv = [[] for _ in range(8)]
    final_w = final_norm_w.reshape(1, d)

    for l in range(depth):
        lambda_init = 0.8 - 0.6 * math.exp(-0.3 * l)
        ada = _ada(c_all, ada_w[l], ada_b[l])
        mods_p = [ada[:batch, i * d:(i + 1) * d].reshape(batch, 1, d) for i in range(6)]
        mods_s = [jnp.repeat(ada[batch:, i * d:(i + 1) * d], n_new, axis=0) for i in range(6)]
        lamv = jnp.stack([diff_lambda_q1[l], diff_lambda_k1[l], diff_lambda_q2[l], diff_lambda_k2[l]]).astype(F32)
        sw = diff_subln_w[l].reshape(1, DIFF_V_DIM).astype(F32)
        nw_a = norm_attn_w[l].reshape(1, d)
        nw_f = norm_ffn_w[l].reshape(1, d)
        w_in_bf = w_in[l].astype(BF16)
        w_out_bf = w_out[l].astype(BF16)
        rw_hi = router_w[l].astype(BF16)
        rw_lo = (router_w[l] - rw_hi.astype(F32)).astype(BF16)
        rb = router_b[l].reshape(1, N_EXPERTS)

        (sbq, sbk, sbv, dq, dk, dv, sbk_t, sbv_t, dk_f, dv_f) = _inproj_prompt(
            hp, mods_p[0], mods_p[1], nw_a, w_in_bf, w_in_bf[:, PIECE:3 * PIECE].T, batch, seq)
        o_sb = _sb_prompt(sbq, sbk, sbv).reshape(n_p, SB_WIDTH)
        o_d = _diff_prompt(lamv, sw, dq, dk, dv, lambda_init).reshape(n_p, DIFF_V_WIDTH)

        p_s = _inproj_sample(hs, mods_s[0], mods_s[1], nw_a, w_in_bf).reshape(nseq, n_new, N_PIECES * PIECE)
        caches = [jnp.transpose(c[l], (0, 2, 3, 1)).reshape(n_pool, SB_WIDTH, page)
                  for c in (cache_sb_k, cache_sb_v)]
        caches += [c[l].reshape(n_pool, page * DIFF_HEADS, DIFF_V_DIM) for c in (cache_diff_k, cache_diff_v)]
        o_s = _sample_attention(page_table, lamv, sw,
                                _sb_query_rows(p_s[:, :, 0:PIECE]),
                                _diff_query_rows(p_s[:, :, 3 * PIECE:4 * PIECE]),
                                p_s, caches, lambda_init).reshape(n_s, 2 * PIECE)

        h1_p, f_p, gates_p = _post(hp, o_sb, o_d, mods_p[2], mods_p[3], mods_p[4], nw_f, w_out_bf,
                                   rw_hi, rw_lo, rb, seq)
        h1_s, f_s, gates_s = _post(hs, o_s[:, :SB_WIDTH], o_s[:, SB_WIDTH:], mods_s[2], mods_s[3], mods_s[4],
                                   nw_f, w_out_bf, rw_hi, rw_lo, rb, None)

        f_out = _moe(jnp.concatenate([f_p, f_s], axis=0), jnp.concatenate([gates_p, gates_s], axis=0),
                     w_gate_up[l].astype(BF16), b_gate_up[l].reshape(N_EXPERTS, 1, -1),
                     w_down[l].astype(BF16), b_down[l].reshape(N_EXPERTS, 1, -1))
        last = l == depth - 1
        hp = _final(h1_p, f_out, 0, mods_p[5], final_w, seq, last)
        hs = _final(h1_s, f_out, n_p, mods_s[5], final_w, None, last)

        p_s_flat = p_s.reshape(n_s, N_PIECES * PIECE)
        new_kv[0].append(jnp.transpose(sbk_t.reshape(batch, SB_HEADS, HEAD_DIM, seq), (0, 3, 1, 2)))
        new_kv[1].append(jnp.transpose(sbv_t.reshape(batch, SB_HEADS, HEAD_DIM, seq), (0, 3, 1, 2)))
        new_kv[2].append(dk_f.reshape(batch, seq, DIFF_HEADS, 2 * HEAD_DIM))
        new_kv[3].append(dv_f.reshape(batch, seq, DIFF_HEADS, DIFF_V_DIM))
        new_kv[4].append(p_s_flat[:, 1 * PIECE:2 * PIECE].reshape(nseq, n_new, SB_HEADS, HEAD_DIM))
        new_kv[5].append(p_s_flat[:, 2 * PIECE:3 * PIECE].reshape(nseq, n_new, SB_HEADS, HEAD_DIM))
        new_kv[6].append(p_s_flat[:, 4 * PIECE:5 * PIECE].reshape(nseq, n_new, DIFF_HEADS, 2 * HEAD_DIM))
        new_kv[7].append(p_s_flat[:, 5 * PIECE:6 * PIECE].reshape(nseq, n_new, DIFF_HEADS, DIFF_V_DIM))

    return (hp.reshape(batch, seq, d), hs.reshape(nseq, n_new, d)) + tuple(jnp.stack(v) for v in new_kv)
```
